```python
import jax, jax.numpy as jnp
from jax import lax
import numpy as np

D_MODEL = 2048
BATCH = 4
SEQ = 2048
DEPTH = 1

CHUNK = 64
D_MIX = D_MODEL
SB_HEADS = 8
SB_HEAD_DIM = 128
SB_WIDTH = SB_HEADS * SB_HEAD_DIM
SB_BLOCK = 128
SG_GROUPS = 8
SG_WIDTH = D_MIX - SB_WIDTH
SG_GROUP_DIM = SG_WIDTH // SG_GROUPS
SG_BLOCK = 128
IN_WIDTH = 3 * SB_WIDTH + 2 * SG_WIDTH
PEER_HEADS = 8
PEER_N_KEYS = 128
PEER_N_EXPERTS = PEER_N_KEYS * PEER_N_KEYS
PEER_KEY_DIM = 128
PEER_TOPK = 16
PEER_TOKEN_BLOCK = 128
RMS_EPS = 1e-6

kernel_name = 'hybrid_stickbreak_gmlp_peer'


def rms_norm(x, gain=None):
    xf = x.astype(jnp.float32)
    y = xf * lax.rsqrt(jnp.mean(xf * xf, axis=-1, keepdims=True) + RMS_EPS)
    if gain is not None:
        y = y * gain.astype(jnp.float32)
    return y.astype(x.dtype)


def stick_breaking_attention(q, k, v):
    b, s, h, d = q.shape
    scale = d ** -0.5
    outs = []
    for i in range(s // SB_BLOCK):
        start, end = i * SB_BLOCK, (i + 1) * SB_BLOCK
        z = jnp.einsum('bqhd,bkhd->bhqk', q[:, start:end], k[:, :end]).astype(jnp.float32) * scale
        t_pos = start + jnp.arange(SB_BLOCK)[:, None]
        s_pos = jnp.arange(end)[None, :]
        strict = s_pos < t_pos
        log_rem = jnp.where(strict, jax.nn.log_sigmoid(-z), 0.0)
        tail = lax.cumsum(log_rem, axis=3, reverse=True) - log_rem
        w = jnp.where(strict, jnp.exp(jax.nn.log_sigmoid(z) + tail), 0.0)
        outs.append(jnp.einsum('bhqk,bkhd->bqhd', w.astype(v.dtype), v[:, :end]))
    return jnp.concatenate(outs, axis=1)


def chunked_spatial_gating(u, v, w_s, b_s, g_v):
    b, s, g, c = u.shape
    n = s // SG_BLOCK
    chunk_id = jnp.arange(SG_BLOCK) // CHUNK
    mask = chunk_id[:, None] >= chunk_id[None, :]
    w = jnp.where(mask[None], w_s, 0.0).astype(v.dtype)
    vb = rms_norm(v, g_v).reshape(b, n, SG_BLOCK, g, c)
    mixed = jnp.einsum('gpq,bnqgc->bnpgc', w, vb) + b_s.T[None, None, :, :, None].astype(v.dtype)
    return u * mixed.reshape(b, s, g, c)


def peer_ffn(h, w_query, sub_keys, expert_u, expert_v):
    b, s, d = h.shape
    t = b * s
    hf = h.reshape(t, d)
    q = rms_norm((hf @ w_query).reshape(t, PEER_HEADS, 2 * PEER_KEY_DIM))
    q = q.reshape(t, PEER_HEADS, 2, PEER_KEY_DIM)
    s1 = jnp.einsum('thk,nk->thn', q[:, :, 0], sub_keys[0]).astype(jnp.float32)
    s2 = jnp.einsum('thk,nk->thn', q[:, :, 1], sub_keys[1]).astype(jnp.float32)
    v1, i1 = lax.top_k(s1, PEER_TOPK)
    v2, i2 = lax.top_k(s2, PEER_TOPK)
    n_cand = PEER_TOPK * PEER_TOPK
    cand_score = (v1[..., :, None] + v2[..., None, :]).reshape(t, PEER_HEADS, n_cand)
    cand_idx = (i1[..., :, None] * PEER_N_KEYS + i2[..., None, :]).reshape(t, PEER_HEADS, n_cand)
    top_score, top_pos = lax.top_k(cand_score, PEER_TOPK)
    expert_idx = jnp.take_along_axis(cand_idx, top_pos, axis=-1)
    gate = jax.nn.softmax(top_score, axis=-1)
    nb = t // PEER_TOKEN_BLOCK
    k_tot = PEER_HEADS * PEER_TOPK

    def block(args):
        hb, idx, gb = args
        u_sel = jnp.take(expert_u, idx, axis=0)
        act = jax.nn.gelu(jnp.einsum('tkd,td->tk', u_sel, hb).astype(jnp.float32), approximate=False) * gb
        v_sel = jnp.take(expert_v, idx, axis=0)
        return jnp.einsum('tk,tkd->td', act.astype(hb.dtype), v_sel)

    out = lax.map(block, (hf.reshape(nb, PEER_TOKEN_BLOCK, d),
                          expert_idx.reshape(nb, PEER_TOKEN_BLOCK, k_tot),
                          gate.reshape(nb, PEER_TOKEN_BLOCK, k_tot)))
    return out.reshape(b, s, d)


def setup_inputs(seed: int = 0) -> dict:
    key = jax.random.key(seed)
    ks = jax.random.split(key, 16)
    f32 = jnp.float32
    nrm = lambda k, shape, scale: jax.random.normal(k, shape, f32) * scale
    gain = lambda k, shape: 1.0 + 0.02 * jax.random.normal(k, shape, f32)
    return {
        'x': jax.random.normal(ks[0], (BATCH, SEQ, D_MODEL), f32),
        'norm_mix': gain(ks[1], (DEPTH, D_MODEL)),
        'w_in': nrm(ks[2], (DEPTH, D_MODEL, IN_WIDTH), D_MODEL ** -0.5),
        'q_norm': gain(ks[3], (DEPTH, SB_HEAD_DIM)),
        'k_norm': gain(ks[4], (DEPTH, SB_HEAD_DIM)),
        'sg_norm': gain(ks[5], (DEPTH, SG_GROUPS, SG_GROUP_DIM)),
        'sg_w': nrm(ks[6], (DEPTH, SG_GROUPS, SG_BLOCK, SG_BLOCK), SG_BLOCK ** -0.5),
        'sg_b': 1.0 + 0.1 * jax.random.normal(ks[7], (DEPTH, SG_GROUPS, SG_BLOCK), f32),
        'out_norm_sb': gain(ks[8], (DEPTH, SB_WIDTH)),
        'out_norm_sg': gain(ks[9], (DEPTH, SG_WIDTH)),
        'w_out': nrm(ks[10], (DEPTH, D_MIX, D_MODEL), D_MIX ** -0.5),
        'norm_ffn': gain(ks[11], (DEPTH, D_MODEL)),
        'peer_w_query': nrm(ks[12], (DEPTH, D_MODEL, PEER_HEADS * 2 * PEER_KEY_DIM), D_MODEL ** -0.5),
        'peer_sub_keys': nrm(ks[13], (DEPTH, 2, PEER_N_KEYS, PEER_KEY_DIM), PEER_KEY_DIM ** -0.5),
        'peer_u': nrm(ks[14], (DEPTH, PEER_N_EXPERTS, D_MODEL), D_MODEL ** -0.5),
        'peer_v': nrm(ks[15], (DEPTH, PEER_N_EXPERTS, D_MODEL), PEER_HEADS ** -0.5),
    }


def reference(x, norm_mix, w_in, q_norm, k_norm, sg_norm, sg_w, sg_b, out_norm_sb, out_norm_sg,
              w_out, norm_ffn, peer_w_query, peer_sub_keys, peer_u, peer_v):
    b, s, _ = x.shape
    for layer in range(DEPTH):
        h = rms_norm(x, norm_mix[layer])
        proj = h @ w_in[layer]
        q, k, v, u_g, v_g = jnp.split(
            proj, [SB_WIDTH, 2 * SB_WIDTH, 3 * SB_WIDTH, 3 * SB_WIDTH + SG_WIDTH], axis=-1)
        q = rms_norm(q.reshape(b, s, SB_HEADS, SB_HEAD_DIM), q_norm[layer])
        k = rms_norm(k.reshape(b, s, SB_HEADS, SB_HEAD_DIM), k_norm[layer])
        v = v.reshape(b, s, SB_HEADS, SB_HEAD_DIM)
        y_sb = stick_breaking_attention(q, k, v).reshape(b, s, SB_WIDTH)
        u_g = jax.nn.gelu(u_g, approximate=False).reshape(b, s, SG_GROUPS, SG_GROUP_DIM)
        v_g = jax.nn.gelu(v_g, approximate=False).reshape(b, s, SG_GROUPS, SG_GROUP_DIM)
        y_sg = chunked_spatial_gating(u_g, v_g, sg_w[layer], sg_b[layer], sg_norm[layer]).reshape(b, s, SG_WIDTH)
        y = jnp.concatenate([rms_norm(y_sb, out_norm_sb[layer]), rms_norm(y_sg, out_norm_sg[layer])], axis=-1)
        x = x + y @ w_out[layer]
        h2 = rms_norm(x, norm_ffn[layer])
        x = x + peer_ffn(h2, peer_w_query[layer], peer_sub_keys[layer], peer_u[layer], peer_v[layer])
    return x
```

```python
import functools
import math

import jax
import jax.numpy as jnp
from jax import lax
from jax.experimental import pallas as pl
from jax.experimental.pallas import tpu as pltpu

F32 = jnp.float32
BF16 = jnp.bfloat16

RMS_EPS = 1e-6
LANES = 128
SUBLANES = 8
CHUNK = 64
SB_HEADS = 8
SG_GROUPS = 8
PEER_HEADS = 8
PEER_KEYS = 128
PEER_TOPK = 16
VMEM_LIMIT = 56 * 1024 * 1024

_NT = (((1,), (1,)), ((), ()))


def _gelu(x):
    return 0.5 * x * (1.0 + lax.erf(x * (1.0 / math.sqrt(2.0))))


def _rms(x):
    return x * lax.rsqrt(jnp.mean(x * x, axis=-1, keepdims=True) + RMS_EPS)


def _inproj_kernel(x_ref, nm_ref, w_ref, qn_ref, kn_ref, sgn_ref, o_ref, h_ref, *, tiles_per_region):
    j = pl.program_id(1)
    tn = o_ref.shape[1]

    @pl.when(j == 0)
    def _():
        h_ref[...] = (_rms(x_ref[...]) * nm_ref[...]).astype(BF16)

    acc = jnp.dot(h_ref[...], w_ref[...], preferred_element_type=F32)
    region = j // tiles_per_region

    def group_norm(a, gain_of):
        for c in range(tn // LANES):
            sl = slice(c * LANES, (c + 1) * LANES)
            o_ref[:, sl] = (_rms(a[:, sl]) * gain_of(sl)).astype(o_ref.dtype)

    @pl.when(region == 0)
    def _():
        group_norm(acc, lambda sl: qn_ref[...])

    @pl.when(region == 1)
    def _():
        group_norm(acc, lambda sl: kn_ref[...])

    @pl.when(region == 2)
    def _():
        o_ref[...] = acc.astype(o_ref.dtype)

    @pl.when(region == 3)
    def _():
        o_ref[...] = _gelu(acc).astype(o_ref.dtype)

    @pl.when(region == 4)
    def _():
        group_norm(_gelu(acc), lambda sl: sgn_ref[:, sl])


def _inproj(x2, norm_mix, w_in, q_norm, k_norm, sg_norm, *, tm=512, tn=512):
    t, d = x2.shape
    n = w_in.shape[1]
    region_w = SB_HEADS * LANES
    tpr = region_w // tn
    return pl.pallas_call(
        functools.partial(_inproj_kernel, tiles_per_region=tpr),
        grid=(t // tm, n // tn),
        in_specs=[
            pl.BlockSpec((tm, d), lambda i, j: (i, 0)),
            pl.BlockSpec((1, d), lambda i, j: (0, 0)),
            pl.BlockSpec((d, tn), lambda i, j: (0, j)),
            pl.BlockSpec((1, LANES), lambda i, j: (0, 0)),
            pl.BlockSpec((1, LANES), lambda i, j: (0, 0)),
            pl.BlockSpec((1, tn), lambda i, j: (0, jnp.maximum(j - 4 * tpr, 0))),
        ],
        out_specs=pl.BlockSpec((tm, tn), lambda i, j: (i, j)),
        out_shape=jax.ShapeDtypeStruct((t, n), BF16),
        scratch_shapes=[pltpu.VMEM((tm, d), BF16)],
        compiler_params=pltpu.CompilerParams(
            dimension_semantics=("parallel", "arbitrary"), vmem_limit_bytes=VMEM_LIMIT),
        name="inproj",
    )(x2, norm_mix, w_in, q_norm, k_norm, sg_norm)


def _attn_kernel(q_ref, k_ref, v_ref, tri_ref, o_ref, *, scale):
    i = pl.program_id(2)
    q = q_ref[0]
    tri = tri_ref[...]
    row = lax.broadcasted_iota(jnp.int32, (LANES, LANES), 0)
    col = lax.broadcasted_iota(jnp.int32, (LANES, LANES), 1)
    strict = col < row

    def block(start, carry, diag):
        acc, c = carry
        kb = k_ref[0, pl.ds(start, LANES), :]
        vb = v_ref[0, pl.ds(start, LANES), :]
        z = lax.dot_general(q, kb, _NT, preferred_element_type=F32) * scale
        lr = -(jnp.maximum(z, 0.0) + jnp.log1p(jnp.exp(-jnp.abs(z))))
        if diag:
            lr = jnp.where(strict, lr, 0.0)
        hi = lr.astype(BF16)
        lo = (lr - hi.astype(F32)).astype(BF16)
        cs = jnp.dot(jnp.concatenate([hi, lo], axis=1), tri, preferred_element_type=F32)
        tail = cs[:, :LANES] + c
        w = jnp.exp(z + lr + tail)
        if diag:
            w = jnp.where(strict, w, 0.0)
        acc = acc + jnp.dot(w.astype(BF16), vb, preferred_element_type=F32)
        return acc, c + cs[:, LANES:]

    zeros = jnp.zeros((LANES, LANES), F32)
    carry = block(pl.multiple_of(i * LANES, LANES), (zeros, zeros), True)

    def body(t, carry):
        return block(pl.multiple_of((i - 1 - t) * LANES, LANES), carry, False)

    acc, _ = lax.fori_loop(0, i, body, carry)
    o_ref[0] = acc.astype(o_ref.dtype)


def _cumsum_matrix():
    j = lax.broadcasted_iota(jnp.int32, (2 * LANES, 2 * LANES), 0) % LANES
    s = lax.broadcasted_iota(jnp.int32, (2 * LANES, 2 * LANES), 1)
    return jnp.where((s >= LANES) | (j > s), 1.0, 0.0).astype(BF16)


def _attn(proj3):
    b, s, _ = proj3.shape
    h = SB_HEADS
    return pl.pallas_call(
        functools.partial(_attn_kernel, scale=LANES ** -0.5),
        grid=(b, h, s // LANES),
        in_specs=[
            pl.BlockSpec((1, LANES, LANES), lambda bi, hi, i: (bi, i, hi)),
            pl.BlockSpec((1, s, LANES), lambda bi, hi, i: (bi, 0, h + hi)),
            pl.BlockSpec((1, s, LANES), lambda bi, hi, i: (bi, 0, 2 * h + hi)),
            pl.BlockSpec((2 * LANES, 2 * LANES), lambda bi, hi, i: (0, 0)),
        ],
        out_specs=pl.BlockSpec((1, LANES, LANES), lambda bi, hi, i: (bi, i, hi)),
        out_shape=jax.ShapeDtypeStruct((b, s, h * LANES), BF16),
        compiler_params=pltpu.CompilerParams(
            dimension_semantics=("parallel", "parallel", "arbitrary"), vmem_limit_bytes=VMEM_LIMIT),
        name="attn",
    )(proj3, proj3, proj3, _cumsum_matrix())


def _mix_out_kernel(ysb_ref, u_ref, vb_ref, sgw_ref, sgb_ref, x_ref, wout_ref, gsb_ref, gsg_ref, nf_ref,
                    x1_ref, h2_ref, ysg_ref):
    tm = x_ref.shape[0]
    p = lax.broadcasted_iota(jnp.int32, (LANES, LANES), 0) // CHUNK
    q = lax.broadcasted_iota(jnp.int32, (LANES, LANES), 1) // CHUNK
    causal = p >= q
    for g in range(SG_GROUPS):
        w = jnp.where(causal, sgw_ref[g], 0.0).astype(BF16)
        cols = slice(g * LANES, (g + 1) * LANES)
        for r in range(tm // LANES):
            rows = slice(r * LANES, (r + 1) * LANES)
            mixed = jnp.dot(w, vb_ref[rows, cols], preferred_element_type=F32) + sgb_ref[g]
            ysg_ref[rows, cols] = u_ref[rows, cols].astype(F32) * mixed
    sbw = ysb_ref.shape[1]
    y_sb = (_rms(ysb_ref[...].astype(F32)) * gsb_ref[...]).astype(BF16)
    y_sg = (_rms(ysg_ref[...]) * gsg_ref[...]).astype(BF16)
    y = jnp.dot(y_sb, wout_ref[:sbw, :], preferred_element_type=F32)
    y = y + jnp.dot(y_sg, wout_ref[sbw:, :], preferred_element_type=F32)
    x1 = x_ref[...] + y
    x1_ref[...] = x1
    h2_ref[...] = (_rms(x1) * nf_ref[...]).astype(BF16)


def _mix_out(y_sb, proj, sg_w, sg_b_bcast, x2, w_out, g_sb, g_sg, norm_ffn, *, tm=256):
    t, d = x2.shape
    sbw = y_sb.shape[1]
    sgw = SG_GROUPS * LANES
    u_blk = (3 * sbw) // sgw
    return pl.pallas_call(
        _mix_out_kernel,
        grid=(t // tm,),
        in_specs=[
            pl.BlockSpec((tm, sbw), lambda i: (i, 0)),
            pl.BlockSpec((tm, sgw), lambda i: (i, u_blk)),
            pl.BlockSpec((tm, sgw), lambda i: (i, u_blk + 1)),
            pl.BlockSpec((SG_GROUPS, LANES, LANES), lambda i: (0, 0, 0)),
            pl.BlockSpec((SG_GROUPS, LANES, LANES), lambda i: (0, 0, 0)),
            pl.BlockSpec((tm, d), lambda i: (i, 0)),
            pl.BlockSpec((sbw + sgw, d), lambda i: (0, 0)),
            pl.BlockSpec((1, sbw), lambda i: (0, 0)),
            pl.BlockSpec((1, sgw), lambda i: (0, 0)),
            pl.BlockSpec((1, d), lambda i: (0, 0)),
        ],
        out_specs=[pl.BlockSpec((tm, d), lambda i: (i, 0)), pl.BlockSpec((tm, d), lambda i: (i, 0))],
        out_shape=[jax.ShapeDtypeStruct((t, d), F32), jax.ShapeDtypeStruct((t, d), BF16)],
        scratch_shapes=[pltpu.VMEM((tm, sgw), F32)],
        compiler_params=pltpu.CompilerParams(
            dimension_semantics=("parallel",), vmem_limit_bytes=VMEM_LIMIT),
        name="mix_out",
    )(y_sb, proj, proj, sg_w, sg_b_bcast, x2, w_out, g_sb, g_sg, norm_ffn)


def _query_kernel(h2_ref, wq_ref, keys_ref, st_ref):
    qf = jnp.dot(h2_ref[...], wq_ref[...], preferred_element_type=F32)
    hw = 2 * PEER_KEYS
    for h in range(PEER_HEADS):
        qn = _rms(qf[:, h * hw:(h + 1) * hw]).astype(BF16)
        for half in range(2):
            qk = qn[:, half * PEER_KEYS:(half + 1) * PEER_KEYS]
            st_ref[2 * h + half] = lax.dot_general(keys_ref[half], qk, _NT, preferred_element_type=F32)


def _query(h2, w_query, sub_keys, *, tm=256):
    t, d = h2.shape
    return pl.pallas_call(
        _query_kernel,
        grid=(t // tm,),
        in_specs=[
            pl.BlockSpec((tm, d), lambda i: (i, 0)),
            pl.BlockSpec(w_query.shape, lambda i: (0, 0)),
            pl.BlockSpec(sub_keys.shape, lambda i: (0, 0, 0)),
        ],
        out_specs=pl.BlockSpec((2 * PEER_HEADS, PEER_KEYS, tm), lambda i: (0, 0, i)),
        out_shape=jax.ShapeDtypeStruct((2 * PEER_HEADS, PEER_KEYS, t), F32),
        compiler_params=pltpu.CompilerParams(
            dimension_semantics=("parallel",), vmem_limit_bytes=VMEM_LIMIT),
        name="query",
    )(h2, w_query, sub_keys)


def _sort_pairs(n):
    pairs = []
    p = 1
    while p < n:
        k = p
        while k >= 1:
            for j in range(k % p, n - k, 2 * k):
                for i in range(min(k, n - j - k)):
                    if (i + j) // (2 * p) == (i + j + k) // (2 * p):
                        pairs.append((i + j, i + j + k))
            k //= 2
        p *= 2
    return pairs


def _exchange(v, lo, hi):
    a, b = v[lo], v[hi]
    if b is None:
        return
    if a is None:
        v[lo], v[hi] = b, None
        return
    v[lo], v[hi] = jnp.maximum(a, b), jnp.minimum(a, b)


def _sort_desc(v):
    v = list(v)
    for lo, hi in _sort_pairs(len(v)):
        _exchange(v, lo, hi)
    return v


def _merge_top(x, y):
    n = len(x)
    v = [jnp.maximum(x[i], y[n - 1 - i]) for i in range(n)]
    k = n // 2
    while k >= 1:
        for i in range(n):
            if i & k == 0:
                _exchange(v, i, i + k)
        k //= 2
    return v


def _top_desc(vals, k):
    groups = [_sort_desc(vals[g:g + k]) for g in range(0, len(vals), k)]
    while len(groups) > 1:
        groups = [_merge_top(groups[g], groups[g + 1]) for g in range(0, len(groups), 2)]
    return groups[0]


def _topk_kernel(s_ref, th_ref, e1_ref, m2_ref):
    k = PEER_TOPK
    a = [s_ref[0, 0, n] for n in range(PEER_KEYS)]
    b = [s_ref[0, 1, n] for n in range(PEER_KEYS)]
    ta = _top_desc(a, k)
    tb = _top_desc(b, k)
    cand = [ta[p] + tb[q] for p in range(k) for q in range(k) if (p + 1) * (q + 1) <= k]
    n_sort = 1 << (len(cand) - 1).bit_length()
    tau = _sort_desc(cand + [None] * (n_sort - len(cand)))[k - 1]
    top = ta[0] + tb[0]
    z = jnp.zeros_like(top)
    for c in cand:
        z = z + jnp.where(c >= tau, jnp.exp(c - top), 0.0)
    inv_z = 1.0 / z
    m2_ref[0] = tb[0]
    for n in range(PEER_KEYS):
        th = jnp.full_like(top, jnp.inf)
        for q in range(k):
            th = jnp.where(a[n] + tb[q] >= tau, tb[q], th)
        th_ref[0, n] = th
        e1_ref[0, n] = jnp.exp(a[n] - ta[0]) * inv_z


def _topk(st5):
    hh, _, nk, nc, _ = st5.shape
    blk = (1, nk, SUBLANES, LANES)
    return pl.pallas_call(
        _topk_kernel,
        grid=(hh, nc // SUBLANES),
        in_specs=[pl.BlockSpec((1, 2, nk, SUBLANES, LANES), lambda h, u: (h, 0, 0, u, 0))],
        out_specs=[
            pl.BlockSpec(blk, lambda h, u: (h, 0, u, 0)),
            pl.BlockSpec(blk, lambda h, u: (h, 0, u, 0)),
            pl.BlockSpec((1, SUBLANES, LANES), lambda h, u: (h, u, 0)),
        ],
        out_shape=[
            jax.ShapeDtypeStruct((hh, nk, nc, LANES), F32),
            jax.ShapeDtypeStruct((hh, nk, nc, LANES), F32),
            jax.ShapeDtypeStruct((hh, nc, LANES), F32),
        ],
        compiler_params=pltpu.CompilerParams(
            dimension_semantics=("parallel", "parallel"), vmem_limit_bytes=VMEM_LIMIT),
        name="topk",
    )(st5)


def _peer_kernel(h2_ref, u_ref, vt_ref, s2_ref, m2_ref, th_ref, e1_ref, o_ref, e2_ref, w_ref):
    i = pl.program_id(0)
    j = pl.program_id(1)
    te, tm = w_ref.shape
    n_chunks = tm // LANES
    chunk0 = (i % (SUBLANES // n_chunks)) * n_chunks

    @pl.when(j == 0)
    def _():
        o_ref[...] = jnp.zeros_like(o_ref)
        for h in range(PEER_HEADS):
            for c in range(n_chunks):
                lanes = slice(c * LANES, (c + 1) * LANES)
                e2_ref[h, :, lanes] = jnp.exp(s2_ref[h, :, lanes] - m2_ref[h, pl.ds(chunk0 + c, 1), :])

    st = lax.dot_general(u_ref[...], h2_ref[...], _NT, preferred_element_type=F32)
    for l in range(te // PEER_KEYS):
        i1 = j * (te // PEER_KEYS) + l
        rows = slice(l * PEER_KEYS, (l + 1) * PEER_KEYS)
        for c in range(n_chunks):
            lanes = slice(c * LANES, (c + 1) * LANES)
            gate = jnp.zeros((PEER_KEYS, LANES), F32)
            for h in range(PEER_HEADS):
                th = th_ref[h, i1, pl.ds(chunk0 + c, 1), :]
                e1 = e1_ref[h, i1, pl.ds(chunk0 + c, 1), :]
                gate = gate + jnp.where(s2_ref[h, :, lanes] >= th, e2_ref[h, :, lanes], 0.0) * e1
            w_ref[rows, lanes] = (_gelu(st[rows, lanes]) * gate).astype(BF16)
    o_ref[...] += jnp.dot(vt_ref[...], w_ref[...], preferred_element_type=F32)


def _peer(h2, u_bf, vt_bf, st4, m2, th, e1, *, tm=512, te=512):
    t, d = h2.shape
    n_exp = u_bf.shape[0]
    nk = PEER_KEYS
    return pl.pallas_call(
        _peer_kernel,
        grid=(t // tm, n_exp // te),
        in_specs=[
            pl.BlockSpec((tm, d), lambda i, j: (i, 0)),
            pl.BlockSpec((te, d), lambda i, j: (j, 0)),
            pl.BlockSpec((d, te), lambda i, j: (0, j)),
            pl.BlockSpec((PEER_HEADS, None, nk, tm), lambda i, j: (0, 1, 0, i)),
            pl.BlockSpec((PEER_HEADS, SUBLANES, LANES), lambda i, j: (0, i * tm // (SUBLANES * LANES), 0)),
            pl.BlockSpec((PEER_HEADS, nk, SUBLANES, LANES), lambda i, j: (0, 0, i * tm // (SUBLANES * LANES), 0)),
            pl.BlockSpec((PEER_HEADS, nk, SUBLANES, LANES), lambda i, j: (0, 0, i * tm // (SUBLANES * LANES), 0)),
        ],
        out_specs=pl.BlockSpec((d, tm), lambda i, j: (0, i)),
        out_shape=jax.ShapeDtypeStruct((d, t), F32),
        scratch_shapes=[pltpu.VMEM((PEER_HEADS, nk, tm), F32), pltpu.VMEM((te, tm), BF16)],
        compiler_params=pltpu.CompilerParams(
            dimension_semantics=("parallel", "arbitrary"), vmem_limit_bytes=VMEM_LIMIT),
        name="peer",
    )(h2, u_bf, vt_bf, st4, m2, th, e1)


def _layer(x, norm_mix, w_in, q_norm, k_norm, sg_norm, sg_w, sg_b, out_norm_sb, out_norm_sg,
           w_out, norm_ffn, peer_w_query, peer_sub_keys, peer_u, peer_v):
    b, s, d = x.shape
    t = b * s
    x2 = x.reshape(t, d)
    row = lambda v: v.reshape(1, -1)

    proj = _inproj(x2, row(norm_mix), w_in.astype(BF16), row(q_norm), row(k_norm), row(sg_norm))
    y_sb = _attn(proj.reshape(b, s, -1)).reshape(t, -1)
    sg_b_bcast = jnp.broadcast_to(sg_b[:, :, None], sg_w.shape)
    x1, h2 = _mix_out(y_sb, proj, sg_w, sg_b_bcast, x2, w_out.astype(BF16),
                      row(out_norm_sb), row(out_norm_sg), row(norm_ffn))

    st = _query(h2, peer_w_query.astype(BF16), peer_sub_keys.astype(BF16))
    st4 = st.reshape(PEER_HEADS, 2, PEER_KEYS, t)
    th, e1, m2 = _topk(st4.reshape(PEER_HEADS, 2, PEER_KEYS, t // LANES, LANES))
    out_t = _peer(h2, peer_u.astype(BF16), peer_v.astype(BF16).T, st4, m2, th, e1)
    return (x1 + out_t.T).reshape(b, s, d)


def kernel(x, norm_mix, w_in, q_norm, k_norm, sg_norm, sg_w, sg_b, out_norm_sb, out_norm_sg,
           w_out, norm_ffn, peer_w_query, peer_sub_keys, peer_u, peer_v):
    for layer in range(norm_mix.shape[0]):
        x = _layer(x, norm_mix[layer], w_in[layer], q_norm[layer], k_norm[layer], sg_norm[layer],
                   sg_w[layer], sg_b[layer], out_norm_sb[layer], out_norm_sg[layer], w_out[layer],
                   norm_ffn[layer], peer_w_query[layer], peer_sub_keys[layer], peer_u[layer],
                   peer_v[layer])
    return x
```

```python
import functools
import math

import jax
import jax.numpy as jnp
from jax import lax
from jax.experimental import pallas as pl
from jax.experimental.pallas import tpu as pltpu

F32 = jnp.float32
BF16 = jnp.bfloat16

RMS_EPS = 1e-6
LANES = 128
SUBLANES = 8
CHUNK = 64
SB_HEADS = 8
SG_GROUPS = 8
PEER_HEADS = 8
PEER_KEYS = 128
PEER_TOPK = 16
VMEM_LIMIT = 56 * 1024 * 1024

_NT = (((1,), (1,)), ((), ()))
_LOG2E = 1.0 / math.log(2.0)


def _gelu(x):
    return 0.5 * x * (1.0 + lax.erf(x * (1.0 / math.sqrt(2.0))))


def _rms(x):
    return x * lax.rsqrt(jnp.mean(x * x, axis=-1, keepdims=True) + RMS_EPS)


def _inproj_kernel(x_ref, nm_ref, w_ref, qn_ref, kn_ref, sgn_ref, o_ref, h_ref, *, tiles_per_region):
    j = pl.program_id(1)
    tn = o_ref.shape[1]

    @pl.when(j == 0)
    def _():
        h_ref[...] = (_rms(x_ref[...]) * nm_ref[...]).astype(BF16)

    acc = jnp.dot(h_ref[...], w_ref[...], preferred_element_type=F32)
    region = j // tiles_per_region

    def group_norm(a, gain_of):
        for c in range(tn // LANES):
            sl = slice(c * LANES, (c + 1) * LANES)
            o_ref[:, sl] = (_rms(a[:, sl]) * gain_of(sl)).astype(o_ref.dtype)

    @pl.when(region == 0)
    def _():
        group_norm(acc, lambda sl: qn_ref[...])

    @pl.when(region == 1)
    def _():
        group_norm(acc, lambda sl: kn_ref[...])

    @pl.when(region == 2)
    def _():
        o_ref[...] = acc.astype(o_ref.dtype)

    @pl.when(region == 3)
    def _():
        o_ref[...] = _gelu(acc).astype(o_ref.dtype)

    @pl.when(region == 4)
    def _():
        group_norm(_gelu(acc), lambda sl: sgn_ref[:, sl])


def _inproj(x2, norm_mix, w_in, q_norm, k_norm, sg_norm, *, tm=512, tn=512):
    t, d = x2.shape
    n = w_in.shape[1]
    region_w = SB_HEADS * LANES
    tpr = region_w // tn
    return pl.pallas_call(
        functools.partial(_inproj_kernel, tiles_per_region=tpr),
        grid=(t // tm, n // tn),
        in_specs=[
            pl.BlockSpec((tm, d), lambda i, j: (i, 0)),
            pl.BlockSpec((1, d), lambda i, j: (0, 0)),
            pl.BlockSpec((d, tn), lambda i, j: (0, j)),
            pl.BlockSpec((1, LANES), lambda i, j: (0, 0)),
            pl.BlockSpec((1, LANES), lambda i, j: (0, 0)),
            pl.BlockSpec((1, tn), lambda i, j: (0, jnp.maximum(j - 4 * tpr, 0))),
        ],
        out_specs=pl.BlockSpec((tm, tn), lambda i, j: (i, j)),
        out_shape=jax.ShapeDtypeStruct((t, n), BF16),
        scratch_shapes=[pltpu.VMEM((tm, d), BF16)],
        compiler_params=pltpu.CompilerParams(
            dimension_semantics=("parallel", "arbitrary"), vmem_limit_bytes=VMEM_LIMIT),
        name="inproj",
    )(x2, norm_mix, w_in, q_norm, k_norm, sg_norm)


def _attn_kernel(q_ref, k_ref, v_ref, tri_ref, o_ref, acc_ref, c_ref, *, scale):
    i = pl.program_id(1)
    row = lax.broadcasted_iota(jnp.int32, (LANES, LANES), 0)
    col = lax.broadcasted_iota(jnp.int32, (LANES, LANES), 1)
    strict = col < row

    heads = range(SB_HEADS)
    cols = [slice(h * LANES, (h + 1) * LANES) for h in heads]

    def block(start, diag):
        zs = [lax.dot_general(q_ref[0, :, cols[h]], k_ref[0, pl.ds(start, LANES), cols[h]], _NT,
                              preferred_element_type=F32) * (scale * _LOG2E) for h in heads]
        sps = [jnp.maximum(z, 0.0) + jnp.log2(1.0 + jnp.exp2(-jnp.abs(z))) for z in zs]
        if diag:
            sps = [jnp.where(strict, sp, 0.0) for sp in sps]
        css = []
        for sp in sps:
            hi = sp.astype(BF16)
            lo = (sp - hi.astype(F32)).astype(BF16)
            css.append(jnp.dot(jnp.concatenate([hi, lo], axis=1), tri_ref[...], preferred_element_type=F32))
        for h in heads:
            vb = v_ref[0, pl.ds(start, LANES), cols[h]]
            if diag:
                w = jnp.where(strict, jnp.exp2(zs[h] - sps[h] - css[h][:, :LANES]), 0.0)
                acc_ref[:, cols[h]] = jnp.dot(w.astype(BF16), vb, preferred_element_type=F32)
                c_ref[:, cols[h]] = css[h][:, LANES:]
            else:
                w = jnp.exp2(zs[h] - sps[h] - (css[h][:, :LANES] + c_ref[:, cols[h]]))
                acc_ref[:, cols[h]] += jnp.dot(w.astype(BF16), vb, preferred_element_type=F32)
                c_ref[:, cols[h]] += css[h][:, LANES:]

    block(pl.multiple_of(i * LANES, LANES), True)

    def body(t, carry):
        block(pl.multiple_of((i - 1 - t) * LANES, LANES), False)
        return carry

    lax.fori_loop(0, i, body, 0)
    o_ref[0] = acc_ref[...].astype(o_ref.dtype)


def _cumsum_matrix():
    j = lax.broadcasted_iota(jnp.int32, (2 * LANES, 2 * LANES), 0) % LANES
    s = lax.broadcasted_iota(jnp.int32, (2 * LANES, 2 * LANES), 1)
    return jnp.where((s >= LANES) | (j > s), 1.0, 0.0).astype(BF16)


def _attn(proj3):
    b, s, _ = proj3.shape
    w = SB_HEADS * LANES
    return pl.pallas_call(
        functools.partial(_attn_kernel, scale=LANES ** -0.5),
        grid=(b, s // LANES),
        in_specs=[
            pl.BlockSpec((1, LANES, w), lambda bi, i: (bi, i, 0)),
            pl.BlockSpec((1, s, w), lambda bi, i: (bi, 0, 1)),
            pl.BlockSpec((1, s, w), lambda bi, i: (bi, 0, 2)),
            pl.BlockSpec((2 * LANES, 2 * LANES), lambda bi, i: (0, 0)),
        ],
        out_specs=pl.BlockSpec((1, LANES, w), lambda bi, i: (bi, i, 0)),
        out_shape=jax.ShapeDtypeStruct((b, s, w), BF16),
        scratch_shapes=[pltpu.VMEM((LANES, w), F32), pltpu.VMEM((LANES, w), F32)],
        compiler_params=pltpu.CompilerParams(
            dimension_semantics=("parallel", "arbitrary"), vmem_limit_bytes=VMEM_LIMIT),
        name="attn",
    )(proj3, proj3, proj3, _cumsum_matrix())


def _mix_out_kernel(ysb_ref, u_ref, vb_ref, sgw_ref, sgb_ref, x_ref, wout_ref, gsb_ref, gsg_ref, nf_ref,
                    x1_ref, h2_ref, ysg_ref):
    tm = x_ref.shape[0]
    p = lax.broadcasted_iota(jnp.int32, (LANES, LANES), 0) // CHUNK
    q = lax.broadcasted_iota(jnp.int32, (LANES, LANES), 1) // CHUNK
    causal = p >= q
    for g in range(SG_GROUPS):
        w = jnp.where(causal, sgw_ref[g], 0.0).astype(BF16)
        cols = slice(g * LANES, (g + 1) * LANES)
        for r in range(tm // LANES):
            rows = slice(r * LANES, (r + 1) * LANES)
            mixed = jnp.dot(w, vb_ref[rows, cols], preferred_element_type=F32) + sgb_ref[g]
            ysg_ref[rows, cols] = u_ref[rows, cols].astype(F32) * mixed
    sbw = ysb_ref.shape[1]
    y_sb = (_rms(ysb_ref[...].astype(F32)) * gsb_ref[...]).astype(BF16)
    y_sg = (_rms(ysg_ref[...]) * gsg_ref[...]).astype(BF16)
    y = jnp.dot(y_sb, wout_ref[:sbw, :], preferred_element_type=F32)
    y = y + jnp.dot(y_sg, wout_ref[sbw:, :], preferred_element_type=F32)
    x1 = x_ref[...] + y
    x1_ref[...] = x1
    h2_ref[...] = (_rms(x1) * nf_ref[...]).astype(BF16)


def _mix_out(y_sb, proj, sg_w, sg_b_bcast, x2, w_out, g_sb, g_sg, norm_ffn, *, tm=256):
    t, d = x2.shape
    sbw = y_sb.shape[1]
    sgw = SG_GROUPS * LANES
    u_blk = (3 * sbw) // sgw
    return pl.pallas_call(
        _mix_out_kernel,
        grid=(t // tm,),
        in_specs=[
            pl.BlockSpec((tm, sbw), lambda i: (i, 0)),
            pl.BlockSpec((tm, sgw), lambda i: (i, u_blk)),
            pl.BlockSpec((tm, sgw), lambda i: (i, u_blk + 1)),
            pl.BlockSpec((SG_GROUPS, LANES, LANES), lambda i: (0, 0, 0)),
            pl.BlockSpec((SG_GROUPS, LANES, LANES), lambda i: (0, 0, 0)),
            pl.BlockSpec((tm, d), lambda i: (i, 0)),
            pl.BlockSpec((sbw + sgw, d), lambda i: (0, 0)),
            pl.BlockSpec((1, sbw), lambda i: (0, 0)),
            pl.BlockSpec((1, sgw), lambda i: (0, 0)),
            pl.BlockSpec((1, d), lambda i: (0, 0)),
        ],
        out_specs=[pl.BlockSpec((tm, d), lambda i: (i, 0)), pl.BlockSpec((tm, d), lambda i: (i, 0))],
        out_shape=[jax.ShapeDtypeStruct((t, d), F32), jax.ShapeDtypeStruct((t, d), BF16)],
        scratch_shapes=[pltpu.VMEM((tm, sgw), F32)],
        compiler_params=pltpu.CompilerParams(
            dimension_semantics=("parallel",), vmem_limit_bytes=VMEM_LIMIT),
        name="mix_out",
    )(y_sb, proj, proj, sg_w, sg_b_bcast, x2, w_out, g_sb, g_sg, norm_ffn)


def _query_kernel(h2_ref, wq_ref, keys_ref, st_ref):
    qf = jnp.dot(h2_ref[...], wq_ref[...], preferred_element_type=F32)
    hw = 2 * PEER_KEYS
    for h in range(PEER_HEADS):
        qn = _rms(qf[:, h * hw:(h + 1) * hw]).astype(BF16)
        for half in range(2):
            qk = qn[:, half * PEER_KEYS:(half + 1) * PEER_KEYS]
            st_ref[2 * h + half] = lax.dot_general(keys_ref[half], qk, _NT, preferred_element_type=F32)


def _query(h2, w_query, sub_keys, *, tm=256):
    t, d = h2.shape
    return pl.pallas_call(
        _query_kernel,
        grid=(t // tm,),
        in_specs=[
            pl.BlockSpec((tm, d), lambda i: (i, 0)),
            pl.BlockSpec(w_query.shape, lambda i: (0, 0)),
            pl.BlockSpec(sub_keys.shape, lambda i: (0, 0, 0)),
        ],
        out_specs=pl.BlockSpec((2 * PEER_HEADS, PEER_KEYS, tm), lambda i: (0, 0, i)),
        out_shape=jax.ShapeDtypeStruct((2 * PEER_HEADS, PEER_KEYS, t), F32),
        compiler_params=pltpu.CompilerParams(
            dimension_semantics=("parallel",), vmem_limit_bytes=VMEM_LIMIT),
        name="query",
    )(h2, w_query, sub_keys)


def _sort_pairs(n):
    pairs = []
    p = 1
    while p < n:
        k = p
        while k >= 1:
            for j in range(k % p, n - k, 2 * k):
                for i in range(min(k, n - j - k)):
                    if (i + j) // (2 * p) == (i + j + k) // (2 * p):
                        pairs.append((i + j, i + j + k))
            k //= 2
        p *= 2
    return pairs


def _exchange(v, lo, hi):
    a, b = v[lo], v[hi]
    if b is None:
        return
    if a is None:
        v[lo], v[hi] = b, None
        return
    v[lo], v[hi] = jnp.maximum(a, b), jnp.minimum(a, b)


def _sort_desc(v):
    v = list(v)
    for lo, hi in _sort_pairs(len(v)):
        _exchange(v, lo, hi)
    return v


def _merge_top(x, y):
    n = len(x)
    v = [jnp.maximum(x[i], y[n - 1 - i]) for i in range(n)]
    k = n // 2
    while k >= 1:
        for i in range(n):
            if i & k == 0:
                _exchange(v, i, i + k)
        k //= 2
    return v


def _top_desc(vals, k):
    groups = [_sort_desc(vals[g:g + k]) for g in range(0, len(vals), k)]
    while len(groups) > 1:
        groups = [_merge_top(groups[g], groups[g + 1]) for g in range(0, len(groups), 2)]
    return groups[0]


def _topk_kernel(s_ref, th_ref, e1_ref, m2_ref):
    k = PEER_TOPK
    a = [s_ref[0, 0, n] for n in range(PEER_KEYS)]
    b = [s_ref[0, 1, n] for n in range(PEER_KEYS)]
    ta = _top_desc(a, k)
    tb = _top_desc(b, k)
    cand = [ta[p] + tb[q] for p in range(k) for q in range(k) if (p + 1) * (q + 1) <= k]
    n_sort = 1 << (len(cand) - 1).bit_length()
    tau = _sort_desc(cand + [None] * (n_sort - len(cand)))[k - 1]
    top = ta[0] + tb[0]
    z = jnp.zeros_like(top)
    for c in cand:
        z = z + jnp.where(c >= tau, jnp.exp(c - top), 0.0)
    inv_z = 1.0 / z
    m2_ref[0] = tb[0]
    for n in range(PEER_KEYS):
        th = jnp.full_like(top, jnp.inf)
        for q in range(k):
            th = jnp.where(a[n] + tb[q] >= tau, tb[q], th)
        th_ref[0, n] = th
        e1_ref[0, n] = jnp.exp(a[n] - ta[0]) * inv_z


def _topk(st5):
    hh, _, nk, nc, _ = st5.shape
    blk = (1, nk, SUBLANES, LANES)
    return pl.pallas_call(
        _topk_kernel,
        grid=(hh, nc // SUBLANES),
        in_specs=[pl.BlockSpec((1, 2, nk, SUBLANES, LANES), lambda h, u: (h, 0, 0, u, 0))],
        out_specs=[
            pl.BlockSpec(blk, lambda h, u: (h, 0, u, 0)),
            pl.BlockSpec(blk, lambda h, u: (h, 0, u, 0)),
            pl.BlockSpec((1, SUBLANES, LANES), lambda h, u: (h, u, 0)),
        ],
        out_shape=[
            jax.ShapeDtypeStruct((hh, nk, nc, LANES), F32),
            jax.ShapeDtypeStruct((hh, nk, nc, LANES), F32),
            jax.ShapeDtypeStruct((hh, nc, LANES), F32),
        ],
        compiler_params=pltpu.CompilerParams(
            dimension_semantics=("parallel", "parallel"), vmem_limit_bytes=VMEM_LIMIT),
        name="topk",
    )(st5)


def _peer_kernel(h2_ref, u_ref, vt_ref, s2_ref, m2_ref, th_ref, e1_ref, o_ref, e2_ref, w_ref):
    i = pl.program_id(0)
    j = pl.program_id(1)
    te, tm = w_ref.shape
    n_chunks = tm // LANES
    chunk0 = (i % (SUBLANES // n_chunks)) * n_chunks

    @pl.when(j == 0)
    def _():
        o_ref[...] = jnp.zeros_like(o_ref)
        for h in range(PEER_HEADS):
            for c in range(n_chunks):
                lanes = slice(c * LANES, (c + 1) * LANES)
                e2_ref[h, :, lanes] = jnp.exp(s2_ref[h, :, lanes] - m2_ref[h, pl.ds(chunk0 + c, 1), :])

    st = lax.dot_general(u_ref[...], h2_ref[...], _NT, preferred_element_type=F32)
    for l in range(te // PEER_KEYS):
        i1 = j * (te // PEER_KEYS) + l
        rows = slice(l * PEER_KEYS, (l + 1) * PEER_KEYS)
        for c in range(n_chunks):
            lanes = slice(c * LANES, (c + 1) * LANES)
            gate = jnp.zeros((PEER_KEYS, LANES), F32)
            for h in range(PEER_HEADS):
                th = th_ref[h, i1, pl.ds(chunk0 + c, 1), :]
                e1 = e1_ref[h, i1, pl.ds(chunk0 + c, 1), :]
                gate = gate + jnp.where(s2_ref[h, :, lanes] >= th, e2_ref[h, :, lanes], 0.0) * e1
            w_ref[rows, lanes] = (_gelu(st[rows, lanes]) * gate).astype(BF16)
    o_ref[...] += jnp.dot(vt_ref[...], w_ref[...], preferred_element_type=F32)


def _peer(h2, u_bf, vt_bf, st4, m2, th, e1, *, tm=512, te=512):
    t, d = h2.shape
    n_exp = u_bf.shape[0]
    nk = PEER_KEYS
    return pl.pallas_call(
        _peer_kernel,
        grid=(t // tm, n_exp // te),
        in_specs=[
            pl.BlockSpec((tm, d), lambda i, j: (i, 0)),
            pl.BlockSpec((te, d), lambda i, j: (j, 0)),
            pl.BlockSpec((d, te), lambda i, j: (0, j)),
            pl.BlockSpec((PEER_HEADS, None, nk, tm), lambda i, j: (0, 1, 0, i)),
            pl.BlockSpec((PEER_HEADS, SUBLANES, LANES), lambda i, j: (0, i * tm // (SUBLANES * LANES), 0)),
            pl.BlockSpec((PEER_HEADS, nk, SUBLANES, LANES), lambda i, j: (0, 0, i * tm // (SUBLANES * LANES), 0)),
            pl.BlockSpec((PEER_HEADS, nk, SUBLANES, LANES), lambda i, j: (0, 0, i * tm // (SUBLANES * LANES), 0)),
        ],
        out_specs=pl.BlockSpec((d, tm), lambda i, j: (0, i)),
        out_shape=jax.ShapeDtypeStruct((d, t), F32),
        scratch_shapes=[pltpu.VMEM((PEER_HEADS, nk, tm), F32), pltpu.VMEM((te, tm), BF16)],
        compiler_params=pltpu.CompilerParams(
            dimension_semantics=("parallel", "arbitrary"), vmem_limit_bytes=VMEM_LIMIT),
        name="peer",
    )(h2, u_bf, vt_bf, st4, m2, th, e1)


def _layer(x, norm_mix, w_in, q_norm, k_norm, sg_norm, sg_w, sg_b, out_norm_sb, out_norm_sg,
           w_out, norm_ffn, peer_w_query, peer_sub_keys, peer_u, peer_v):
    b, s, d = x.shape
    t = b * s
    x2 = x.reshape(t, d)
    row = lambda v: v.reshape(1, -1)

    proj = _inproj(x2, row(norm_mix), w_in.astype(BF16), row(q_norm), row(k_norm), row(sg_norm))
    y_sb = _attn(proj.reshape(b, s, -1)).reshape(t, -1)
    sg_b_bcast = jnp.broadcast_to(sg_b[:, :, None], sg_w.shape)
    x1, h2 = _mix_out(y_sb, proj, sg_w, sg_b_bcast, x2, w_out.astype(BF16),
                      row(out_norm_sb), row(out_norm_sg), row(norm_ffn))

    st = _query(h2, peer_w_query.astype(BF16), peer_sub_keys.astype(BF16))
    st4 = st.reshape(PEER_HEADS, 2, PEER_KEYS, t)
    th, e1, m2 = _topk(st4.reshape(PEER_HEADS, 2, PEER_KEYS, t // LANES, LANES))
    out_t = _peer(h2, peer_u.astype(BF16), peer_v.astype(BF16).T, st4, m2, th, e1)
    return (x1 + out_t.T).reshape(b, s, d)


def kernel(x, norm_mix, w_in, q_norm, k_norm, sg_norm, sg_w, sg_b, out_norm_sb, out_norm_sg,
           w_out, norm_ffn, peer_w_query, peer_sub_keys, peer_u, peer_v):
    for layer in range(norm_mix.shape[0]):
        x = _layer(x, norm_mix[layer], w_in[layer], q_norm[layer], k_norm[layer], sg_norm[layer],
                   sg_w[layer], sg_b[layer], out_norm_sb[layer], out_norm_sg[layer], w_out[layer],
                   norm_ffn[layer], peer_w_query[layer], peer_sub_keys[layer], peer_u[layer],
                   peer_v[layer])
    return x
```

```python
import functools
import math

import jax
import jax.numpy as jnp
from jax import lax
from jax.experimental import pallas as pl
from jax.experimental.pallas import tpu as pltpu

F32 = jnp.float32
BF16 = jnp.bfloat16

RMS_EPS = 1e-6
LANES = 128
SUBLANES = 8
CHUNK = 64
SB_HEADS = 8
SG_GROUPS = 8
PEER_HEADS = 8
PEER_KEYS = 128
PEER_TOPK = 16
VMEM_LIMIT = 56 * 1024 * 1024

_NT = (((1,), (1,)), ((), ()))
_LOG2E = 1.0 / math.log(2.0)


def _gelu(x):
    return 0.5 * x * (1.0 + lax.erf(x * (1.0 / math.sqrt(2.0))))


def _rms(x):
    return x * lax.rsqrt(jnp.mean(x * x, axis=-1, keepdims=True) + RMS_EPS)


def _inproj_kernel(x_ref, nm_ref, w_ref, qn_ref, kn_ref, sgn_ref, o_ref, h_ref, *, tiles_per_region):
    j = pl.program_id(1)
    tn = o_ref.shape[1]

    @pl.when(j == 0)
    def _():
        h_ref[...] = (_rms(x_ref[...]) * nm_ref[...]).astype(BF16)

    acc = jnp.dot(h_ref[...], w_ref[...], preferred_element_type=F32)
    region = j // tiles_per_region

    def group_norm(a, gain_of):
        for c in range(tn // LANES):
            sl = slice(c * LANES, (c + 1) * LANES)
            o_ref[:, sl] = (_rms(a[:, sl]) * gain_of(sl)).astype(o_ref.dtype)

    @pl.when(region == 0)
    def _():
        group_norm(acc, lambda sl: qn_ref[...])

    @pl.when(region == 1)
    def _():
        group_norm(acc, lambda sl: kn_ref[...])

    @pl.when(region == 2)
    def _():
        o_ref[...] = acc.astype(o_ref.dtype)

    @pl.when(region == 3)
    def _():
        o_ref[...] = _gelu(acc).astype(o_ref.dtype)

    @pl.when(region == 4)
    def _():
        group_norm(_gelu(acc), lambda sl: sgn_ref[:, sl])


def _inproj(x2, norm_mix, w_in, q_norm, k_norm, sg_norm, *, tm=512, tn=512):
    t, d = x2.shape
    n = w_in.shape[1]
    region_w = SB_HEADS * LANES
    tpr = region_w // tn
    return pl.pallas_call(
        functools.partial(_inproj_kernel, tiles_per_region=tpr),
        grid=(t // tm, n // tn),
        in_specs=[
            pl.BlockSpec((tm, d), lambda i, j: (i, 0)),
            pl.BlockSpec((1, d), lambda i, j: (0, 0)),
            pl.BlockSpec((d, tn), lambda i, j: (0, j)),
            pl.BlockSpec((1, LANES), lambda i, j: (0, 0)),
            pl.BlockSpec((1, LANES), lambda i, j: (0, 0)),
            pl.BlockSpec((1, tn), lambda i, j: (0, jnp.maximum(j - 4 * tpr, 0))),
        ],
        out_specs=pl.BlockSpec((tm, tn), lambda i, j: (i, j)),
        out_shape=jax.ShapeDtypeStruct((t, n), BF16),
        scratch_shapes=[pltpu.VMEM((tm, d), BF16)],
        compiler_params=pltpu.CompilerParams(
            dimension_semantics=("parallel", "arbitrary"), vmem_limit_bytes=VMEM_LIMIT),
        name="inproj",
    )(x2, norm_mix, w_in, q_norm, k_norm, sg_norm)


def _attn_kernel(q_ref, k_ref, v_ref, tri_ref, o_ref, acc_ref, c_ref, *, scale):
    i = pl.program_id(1)
    row = lax.broadcasted_iota(jnp.int32, (LANES, LANES), 0)
    col = lax.broadcasted_iota(jnp.int32, (LANES, LANES), 1)
    strict = col < row

    heads = range(SB_HEADS)
    cols = [slice(h * LANES, (h + 1) * LANES) for h in heads]

    def block(start, diag):
        zs = [lax.dot_general(q_ref[0, :, cols[h]], k_ref[0, pl.ds(start, LANES), cols[h]], _NT,
                              preferred_element_type=F32) * (scale * _LOG2E) for h in heads]
        sps = [jnp.maximum(z, 0.0) + jnp.log2(1.0 + jnp.exp2(-jnp.abs(z))) for z in zs]
        if diag:
            sps = [jnp.where(strict, sp, 0.0) for sp in sps]
        css = []
        for sp in sps:
            hi = sp.astype(BF16)
            lo = (sp - hi.astype(F32)).astype(BF16)
            css.append(jnp.dot(jnp.concatenate([hi, lo], axis=1), tri_ref[...], preferred_element_type=F32))
        for h in heads:
            vb = v_ref[0, pl.ds(start, LANES), cols[h]]
            if diag:
                w = jnp.where(strict, jnp.exp2(zs[h] - sps[h] - css[h][:, :LANES]), 0.0)
                acc_ref[:, cols[h]] = jnp.dot(w.astype(BF16), vb, preferred_element_type=F32)
                c_ref[:, cols[h]] = css[h][:, LANES:]
            else:
                w = jnp.exp2(zs[h] - sps[h] - (css[h][:, :LANES] + c_ref[:, cols[h]]))
                acc_ref[:, cols[h]] += jnp.dot(w.astype(BF16), vb, preferred_element_type=F32)
                c_ref[:, cols[h]] += css[h][:, LANES:]

    block(pl.multiple_of(i * LANES, LANES), True)

    def body(t, carry):
        block(pl.multiple_of((i - 1 - t) * LANES, LANES), False)
        return carry

    lax.fori_loop(0, i, body, 0)
    o_ref[0] = acc_ref[...].astype(o_ref.dtype)


def _cumsum_matrix():
    j = lax.broadcasted_iota(jnp.int32, (2 * LANES, 2 * LANES), 0) % LANES
    s = lax.broadcasted_iota(jnp.int32, (2 * LANES, 2 * LANES), 1)
    return jnp.where((s >= LANES) | (j > s), 1.0, 0.0).astype(BF16)


def _attn(proj3):
    b, s, _ = proj3.shape
    w = SB_HEADS * LANES
    return pl.pallas_call(
        functools.partial(_attn_kernel, scale=LANES ** -0.5),
        grid=(b, s // LANES),
        in_specs=[
            pl.BlockSpec((1, LANES, w), lambda bi, i: (bi, i, 0)),
            pl.BlockSpec((1, s, w), lambda bi, i: (bi, 0, 1)),
            pl.BlockSpec((1, s, w), lambda bi, i: (bi, 0, 2)),
            pl.BlockSpec((2 * LANES, 2 * LANES), lambda bi, i: (0, 0)),
        ],
        out_specs=pl.BlockSpec((1, LANES, w), lambda bi, i: (bi, i, 0)),
        out_shape=jax.ShapeDtypeStruct((b, s, w), BF16),
        scratch_shapes=[pltpu.VMEM((LANES, w), F32), pltpu.VMEM((LANES, w), F32)],
        compiler_params=pltpu.CompilerParams(
            dimension_semantics=("parallel", "arbitrary"), vmem_limit_bytes=VMEM_LIMIT),
        name="attn",
    )(proj3, proj3, proj3, _cumsum_matrix())


def _mix_out_kernel(ysb_ref, u_ref, vb_ref, sgw_ref, sgb_ref, x_ref, wout_ref, gsb_ref, gsg_ref, nf_ref,
                    x1_ref, h2_ref, ysg_ref):
    tm = x_ref.shape[0]
    p = lax.broadcasted_iota(jnp.int32, (LANES, LANES), 0) // CHUNK
    q = lax.broadcasted_iota(jnp.int32, (LANES, LANES), 1) // CHUNK
    causal = p >= q
    for g in range(SG_GROUPS):
        w = jnp.where(causal, sgw_ref[g], 0.0).astype(BF16)
        cols = slice(g * LANES, (g + 1) * LANES)
        for r in range(tm // LANES):
            rows = slice(r * LANES, (r + 1) * LANES)
            mixed = jnp.dot(w, vb_ref[rows, cols], preferred_element_type=F32) + sgb_ref[g]
            ysg_ref[rows, cols] = u_ref[rows, cols].astype(F32) * mixed
    sbw = ysb_ref.shape[1]
    y_sb = (_rms(ysb_ref[...].astype(F32)) * gsb_ref[...]).astype(BF16)
    y_sg = (_rms(ysg_ref[...]) * gsg_ref[...]).astype(BF16)
    y = jnp.dot(y_sb, wout_ref[:sbw, :], preferred_element_type=F32)
    y = y + jnp.dot(y_sg, wout_ref[sbw:, :], preferred_element_type=F32)
    x1 = x_ref[...] + y
    x1_ref[...] = x1
    h2_ref[...] = (_rms(x1) * nf_ref[...]).astype(BF16)


def _mix_out(y_sb, proj, sg_w, sg_b_bcast, x2, w_out, g_sb, g_sg, norm_ffn, *, tm=256):
    t, d = x2.shape
    sbw = y_sb.shape[1]
    sgw = SG_GROUPS * LANES
    u_blk = (3 * sbw) // sgw
    return pl.pallas_call(
        _mix_out_kernel,
        grid=(t // tm,),
        in_specs=[
            pl.BlockSpec((tm, sbw), lambda i: (i, 0)),
            pl.BlockSpec((tm, sgw), lambda i: (i, u_blk)),
            pl.BlockSpec((tm, sgw), lambda i: (i, u_blk + 1)),
            pl.BlockSpec((SG_GROUPS, LANES, LANES), lambda i: (0, 0, 0)),
            pl.BlockSpec((SG_GROUPS, LANES, LANES), lambda i: (0, 0, 0)),
            pl.BlockSpec((tm, d), lambda i: (i, 0)),
            pl.BlockSpec((sbw + sgw, d), lambda i: (0, 0)),
            pl.BlockSpec((1, sbw), lambda i: (0, 0)),
            pl.BlockSpec((1, sgw), lambda i: (0, 0)),
            pl.BlockSpec((1, d), lambda i: (0, 0)),
        ],
        out_specs=[pl.BlockSpec((tm, d), lambda i: (i, 0)), pl.BlockSpec((tm, d), lambda i: (i, 0))],
        out_shape=[jax.ShapeDtypeStruct((t, d), F32), jax.ShapeDtypeStruct((t, d), BF16)],
        scratch_shapes=[pltpu.VMEM((tm, sgw), F32)],
        compiler_params=pltpu.CompilerParams(
            dimension_semantics=("parallel",), vmem_limit_bytes=VMEM_LIMIT),
        name="mix_out",
    )(y_sb, proj, proj, sg_w, sg_b_bcast, x2, w_out, g_sb, g_sg, norm_ffn)


def _query_kernel(h2_ref, wq_ref, keys_ref, st_ref):
    qf = jnp.dot(h2_ref[...], wq_ref[...], preferred_element_type=F32)
    hw = 2 * PEER_KEYS
    for h in range(PEER_HEADS):
        qn = _rms(qf[:, h * hw:(h + 1) * hw]).astype(BF16)
        for half in range(2):
            qk = qn[:, half * PEER_KEYS:(half + 1) * PEER_KEYS]
            st_ref[2 * h + half] = lax.dot_general(keys_ref[half], qk, _NT, preferred_element_type=F32)


def _query(h2, w_query, sub_keys, *, tm=256):
    t, d = h2.shape
    return pl.pallas_call(
        _query_kernel,
        grid=(t // tm,),
        in_specs=[
            pl.BlockSpec((tm, d), lambda i: (i, 0)),
            pl.BlockSpec(w_query.shape, lambda i: (0, 0)),
            pl.BlockSpec(sub_keys.shape, lambda i: (0, 0, 0)),
        ],
        out_specs=pl.BlockSpec((2 * PEER_HEADS, PEER_KEYS, tm), lambda i: (0, 0, i)),
        out_shape=jax.ShapeDtypeStruct((2 * PEER_HEADS, PEER_KEYS, t), F32),
        compiler_params=pltpu.CompilerParams(
            dimension_semantics=("parallel",), vmem_limit_bytes=VMEM_LIMIT),
        name="query",
    )(h2, w_query, sub_keys)


def _sort_pairs(n):
    pairs = []
    p = 1
    while p < n:
        k = p
        while k >= 1:
            for j in range(k % p, n - k, 2 * k):
                for i in range(min(k, n - j - k)):
                    if (i + j) // (2 * p) == (i + j + k) // (2 * p):
                        pairs.append((i + j, i + j + k))
            k //= 2
        p *= 2
    return pairs


def _exchange(v, lo, hi):
    a, b = v[lo], v[hi]
    if b is None:
        return
    if a is None:
        v[lo], v[hi] = b, None
        return
    v[lo], v[hi] = jnp.maximum(a, b), jnp.minimum(a, b)


def _sort_desc(v):
    v = list(v)
    for lo, hi in _sort_pairs(len(v)):
        _exchange(v, lo, hi)
    return v


def _merge_top(x, y):
    n = len(x)
    v = [jnp.maximum(x[i], y[n - 1 - i]) for i in range(n)]
    k = n // 2
    while k >= 1:
        for i in range(n):
            if i & k == 0:
                _exchange(v, i, i + k)
        k //= 2
    return v


def _top_desc(vals, k):
    groups = [_sort_desc(vals[g:g + k]) for g in range(0, len(vals), k)]
    while len(groups) > 1:
        groups = [_merge_top(groups[g], groups[g + 1]) for g in range(0, len(groups), 2)]
    return groups[0]


def _topk_kernel(s_ref, th_ref, e1_ref, m2_ref):
    k = PEER_TOPK
    a = [s_ref[0, 0, n] for n in range(PEER_KEYS)]
    b = [s_ref[0, 1, n] for n in range(PEER_KEYS)]
    ta = _top_desc(a, k)
    tb = _top_desc(b, k)
    cand = [ta[p] + tb[q] for p in range(k) for q in range(k) if (p + 1) * (q + 1) <= k]
    n_sort = 1 << (len(cand) - 1).bit_length()
    tau = _sort_desc(cand + [None] * (n_sort - len(cand)))[k - 1]
    top = ta[0] + tb[0]
    z = jnp.zeros_like(top)
    for c in cand:
        z = z + jnp.where(c >= tau, jnp.exp(c - top), 0.0)
    half_inv_z = 0.5 / z
    m2_ref[0] = tb[0]
    for n in range(PEER_KEYS):
        th = jnp.full_like(top, jnp.inf)
        for q in range(k):
            th = jnp.where(a[n] + tb[q] >= tau, tb[q], th)
        th_ref[0, n] = th
        e1_ref[0, n] = jnp.exp(a[n] - ta[0]) * half_inv_z


def _topk(st5):
    hh, _, nk, nc, _ = st5.shape
    blk = (1, nk, SUBLANES, LANES)
    return pl.pallas_call(
        _topk_kernel,
        grid=(hh, nc // SUBLANES),
        in_specs=[pl.BlockSpec((1, 2, nk, SUBLANES, LANES), lambda h, u: (h, 0, 0, u, 0))],
        out_specs=[
            pl.BlockSpec(blk, lambda h, u: (h, 0, u, 0)),
            pl.BlockSpec(blk, lambda h, u: (h, 0, u, 0)),
            pl.BlockSpec((1, SUBLANES, LANES), lambda h, u: (h, u, 0)),
        ],
        out_shape=[
            jax.ShapeDtypeStruct((hh, nk, nc, LANES), F32),
            jax.ShapeDtypeStruct((hh, nk, nc, LANES), F32),
            jax.ShapeDtypeStruct((hh, nc, LANES), F32),
        ],
        compiler_params=pltpu.CompilerParams(
            dimension_semantics=("parallel", "parallel"), vmem_limit_bytes=VMEM_LIMIT),
        name="topk",
    )(st5)


_PACK_ROWS = 16


def _peer_kernel(h2_ref, u_ref, vt_ref, s2_ref, m2_ref, th_ref, e1_ref, o_ref,
                 st0_ref, st1_ref, w0_ref, w1_ref, e2_ref, *, nj, n_tiles):
    s = pl.program_id(0)
    st_ref = (st0_ref, st1_ref)
    w_ref = (w0_ref, w1_ref)
    te, tm = w0_ref.shape
    n_chunks = tm // LANES
    n_l = te // PEER_KEYS
    heads = range(PEER_HEADS)
    sb = jnp.clip(s - 1, 0, n_tiles - 1)
    sc = jnp.clip(s - 2, 0, n_tiles - 1)
    ib, jb = sb // nj, sb % nj
    chunk0 = (ib % (SUBLANES // n_chunks)) * n_chunks

    @pl.when(s == 0)
    def _():
        st1_ref[...] = jnp.zeros_like(st1_ref)
        w0_ref[...] = jnp.zeros_like(w0_ref)

    @pl.when(jb == 0)
    def _():
        for h in heads:
            for c in range(n_chunks):
                lanes = slice(c * LANES, (c + 1) * LANES)
                e2_ref[h, :, lanes] = jnp.exp(s2_ref[h, :, lanes] - m2_ref[h, pl.ds(chunk0 + c, 1), :])

    @pl.when(sc % nj == 0)
    def _():
        o_ref[...] = jnp.zeros_like(o_ref)

    d = o_ref.shape[0]
    halves = (0, 1)

    def step(par):
        def scores(mh, nt):
            m = slice(mh * (te // 2), (mh + 1) * (te // 2))
            n = slice(nt * (tm // 2), (nt + 1) * (tm // 2))
            st_ref[par][m, n] = lax.dot_general(u_ref[m, :], h2_ref[n, :], _NT, preferred_element_type=F32)

        def accumulate(mh, nt):
            m = slice(mh * (d // 2), (mh + 1) * (d // 2))
            n = slice(nt * (tm // 2), (nt + 1) * (tm // 2))
            o_ref[m, n] += jnp.dot(vt_ref[m, :], w_ref[par][:, n], preferred_element_type=F32)

        def gate_rows(c, r):
            lanes = slice(c * LANES, (c + 1) * LANES)
            keys = slice(r * _PACK_ROWS, (r + 1) * _PACK_ROWS)
            gates = [None] * n_l
            for h in heads:
                s2 = s2_ref[h, keys, lanes]
                e2 = e2_ref[h, keys, lanes]
                for l in range(n_l):
                    i1 = jb * n_l + l
                    th = th_ref[h, i1, pl.ds(chunk0 + c, 1), :]
                    e1 = e1_ref[h, i1, pl.ds(chunk0 + c, 1), :]
                    term = jnp.where(s2 >= th, e2, 0.0) * e1
                    gates[l] = term if gates[l] is None else gates[l] + term
            for l in range(n_l):
                rows = slice(l * PEER_KEYS + r * _PACK_ROWS, l * PEER_KEYS + (r + 1) * _PACK_ROWS)
                a = st_ref[1 - par][rows, lanes]
                w_ref[1 - par][rows, lanes] = (
                    a * (1.0 + lax.erf(a * (1.0 / math.sqrt(2.0)))) * gates[l]).astype(BF16)

        mxu_work = []
        for nt in halves:
            for mh in halves:
                mxu_work += [functools.partial(scores, mh, nt), functools.partial(accumulate, mh, nt)]
        vpu_work = [functools.partial(gate_rows, c, r) for r in range(PEER_KEYS // _PACK_ROWS)
                    for c in range(n_chunks)]
        per = len(vpu_work) // len(mxu_work)
        for k, mm in enumerate(mxu_work):
            mm()
            for blk in vpu_work[k * per:(k + 1) * per]:
                blk()

    @pl.when(s % 2 == 0)
    def _():
        step(0)

    @pl.when(s % 2 == 1)
    def _():
        step(1)


def _peer(h2, u_bf, vt_bf, st4, m2, th, e1, *, tm=512, te=512):
    t, d = h2.shape
    n_exp = u_bf.shape[0]
    nk = PEER_KEYS
    nj = n_exp // te
    n_tiles = (t // tm) * nj
    tile_a = lambda s: jnp.minimum(s, n_tiles - 1)
    tile_b = lambda s: jnp.clip(s - 1, 0, n_tiles - 1)
    tile_c = lambda s: jnp.clip(s - 2, 0, n_tiles - 1)
    big = lambda s: (tile_b(s) // nj) * tm // (SUBLANES * LANES)
    return pl.pallas_call(
        functools.partial(_peer_kernel, nj=nj, n_tiles=n_tiles),
        grid=(n_tiles + 2,),
        in_specs=[
            pl.BlockSpec((tm, d), lambda s: (tile_a(s) // nj, 0)),
            pl.BlockSpec((te, d), lambda s: (tile_a(s) % nj, 0)),
            pl.BlockSpec((d, te), lambda s: (0, tile_c(s) % nj)),
            pl.BlockSpec((PEER_HEADS, None, nk, tm), lambda s: (0, 1, 0, tile_b(s) // nj)),
            pl.BlockSpec((PEER_HEADS, SUBLANES, LANES), lambda s: (0, big(s), 0)),
            pl.BlockSpec((PEER_HEADS, nk, SUBLANES, LANES), lambda s: (0, 0, big(s), 0)),
            pl.BlockSpec((PEER_HEADS, nk, SUBLANES, LANES), lambda s: (0, 0, big(s), 0)),
        ],
        out_specs=pl.BlockSpec((d, tm), lambda s: (0, tile_c(s) // nj)),
        out_shape=jax.ShapeDtypeStruct((d, t), F32),
        scratch_shapes=[pltpu.VMEM((te, tm), F32), pltpu.VMEM((te, tm), F32),
                        pltpu.VMEM((te, tm), BF16), pltpu.VMEM((te, tm), BF16),
                        pltpu.VMEM((PEER_HEADS, nk, tm), F32)],
        compiler_params=pltpu.CompilerParams(
            dimension_semantics=("arbitrary",), vmem_limit_bytes=VMEM_LIMIT),
        name="peer",
    )(h2, u_bf, vt_bf, st4, m2, th, e1)


def _layer(x, norm_mix, w_in, q_norm, k_norm, sg_norm, sg_w, sg_b, out_norm_sb, out_norm_sg,
           w_out, norm_ffn, peer_w_query, peer_sub_keys, peer_u, peer_v):
    b, s, d = x.shape
    t = b * s
    x2 = x.reshape(t, d)
    row = lambda v: v.reshape(1, -1)

    proj = _inproj(x2, row(norm_mix), w_in.astype(BF16), row(q_norm), row(k_norm), row(sg_norm))
    y_sb = _attn(proj.reshape(b, s, -1)).reshape(t, -1)
    sg_b_bcast = jnp.broadcast_to(sg_b[:, :, None], sg_w.shape)
    x1, h2 = _mix_out(y_sb, proj, sg_w, sg_b_bcast, x2, w_out.astype(BF16),
                      row(out_norm_sb), row(out_norm_sg), row(norm_ffn))

    st = _query(h2, peer_w_query.astype(BF16), peer_sub_keys.astype(BF16))
    st4 = st.reshape(PEER_HEADS, 2, PEER_KEYS, t)
    th, e1, m2 = _topk(st4.reshape(PEER_HEADS, 2, PEER_KEYS, t // LANES, LANES))
    out_t = _peer(h2, peer_u.astype(BF16), peer_v.astype(BF16).T, st4, m2, th, e1)
    return (x1 + out_t.T).reshape(b, s, d)


def kernel(x, norm_mix, w_in, q_norm, k_norm, sg_norm, sg_w, sg_b, out_norm_sb, out_norm_sg,
           w_out, norm_ffn, peer_w_query, peer_sub_keys, peer_u, peer_v):
    for layer in range(norm_mix.shape[0]):
        x = _layer(x, norm_mix[layer], w_in[layer], q_norm[layer], k_norm[layer], sg_norm[layer],
                   sg_w[layer], sg_b[layer], out_norm_sb[layer], out_norm_sg[layer], w_out[layer],
                   norm_ffn[layer], peer_w_query[layer], peer_sub_keys[layer], peer_u[layer],
                   peer_v[layer])
    return x
```

```python
import functools
import math

import jax
import jax.numpy as jnp
from jax import lax
from jax.experimental import pallas as pl
from jax.experimental.pallas import tpu as pltpu

F32 = jnp.float32
BF16 = jnp.bfloat16

RMS_EPS = 1e-6
LANES = 128
SUBLANES = 8
CHUNK = 64
SB_HEADS = 8
SG_GROUPS = 8
PEER_HEADS = 8
PEER_KEYS = 128
PEER_TOPK = 16
VMEM_LIMIT = 56 * 1024 * 1024

_NT = (((1,), (1,)), ((), ()))
_LOG2E = 1.0 / math.log(2.0)


def _gelu(x):
    return 0.5 * x * (1.0 + lax.erf(x * (1.0 / math.sqrt(2.0))))


def _rms(x):
    return x * lax.rsqrt(jnp.mean(x * x, axis=-1, keepdims=True) + RMS_EPS)


def _inproj_kernel(x_ref, nm_ref, w_ref, qn_ref, kn_ref, sgn_ref, o_ref, h_ref, *, tiles_per_region):
    j = pl.program_id(1)
    tn = o_ref.shape[1]

    @pl.when(j == 0)
    def _():
        h_ref[...] = (_rms(x_ref[...]) * nm_ref[...]).astype(BF16)

    region = j // tiles_per_region
    mxu_n = 2 * LANES

    def sweep(epilogue):
        for c0 in range(0, tn, mxu_n):
            acc = jnp.dot(h_ref[...], w_ref[:, c0:c0 + mxu_n], preferred_element_type=F32)
            for c in range(c0, c0 + mxu_n, LANES):
                sl = slice(c, c + LANES)
                o_ref[:, sl] = epilogue(acc[:, c - c0:c - c0 + LANES], sl).astype(o_ref.dtype)

    @pl.when(region == 0)
    def _():
        sweep(lambda a, sl: _rms(a) * qn_ref[...])

    @pl.when(region == 1)
    def _():
        sweep(lambda a, sl: _rms(a) * kn_ref[...])

    @pl.when(region == 2)
    def _():
        sweep(lambda a, sl: a)

    @pl.when(region == 3)
    def _():
        sweep(lambda a, sl: _gelu(a))

    @pl.when(region == 4)
    def _():
        sweep(lambda a, sl: _rms(_gelu(a)) * sgn_ref[:, sl])


def _inproj(x2, norm_mix, w_in, q_norm, k_norm, sg_norm, *, tm=512, tn=1024):
    t, d = x2.shape
    n = w_in.shape[1]
    region_w = SB_HEADS * LANES
    tpr = region_w // tn
    return pl.pallas_call(
        functools.partial(_inproj_kernel, tiles_per_region=tpr),
        grid=(t // tm, n // tn),
        in_specs=[
            pl.BlockSpec((tm, d), lambda i, j: (i, 0)),
            pl.BlockSpec((1, d), lambda i, j: (0, 0)),
            pl.BlockSpec((d, tn), lambda i, j: (0, j)),
            pl.BlockSpec((1, LANES), lambda i, j: (0, 0)),
            pl.BlockSpec((1, LANES), lambda i, j: (0, 0)),
            pl.BlockSpec((1, tn), lambda i, j: (0, jnp.maximum(j - 4 * tpr, 0))),
        ],
        out_specs=pl.BlockSpec((tm, tn), lambda i, j: (i, j)),
        out_shape=jax.ShapeDtypeStruct((t, n), BF16),
        scratch_shapes=[pltpu.VMEM((tm, d), BF16)],
        compiler_params=pltpu.CompilerParams(
            dimension_semantics=("parallel", "arbitrary"), vmem_limit_bytes=VMEM_LIMIT),
        name="inproj",
    )(x2, norm_mix, w_in, q_norm, k_norm, sg_norm)


def _attn_kernel(q_ref, k_ref, v_ref, tri_ref, o_ref, acc_ref, c_ref, *, scale):
    i = pl.program_id(1)
    row = lax.broadcasted_iota(jnp.int32, (LANES, LANES), 0)
    col = lax.broadcasted_iota(jnp.int32, (LANES, LANES), 1)
    strict = col < row

    heads = range(SB_HEADS)
    cols = [slice(h * LANES, (h + 1) * LANES) for h in heads]

    def block(start, diag):
        zs = [lax.dot_general(q_ref[0, :, cols[h]], k_ref[0, pl.ds(start, LANES), cols[h]], _NT,
                              preferred_element_type=F32) * (scale * _LOG2E) for h in heads]
        sps = [jnp.maximum(z, 0.0) + jnp.log2(1.0 + jnp.exp2(-jnp.abs(z))) for z in zs]
        if diag:
            sps = [jnp.where(strict, sp, 0.0) for sp in sps]
        css = []
        for sp in sps:
            hi = sp.astype(BF16)
            lo = (sp - hi.astype(F32)).astype(BF16)
            css.append(jnp.dot(jnp.concatenate([hi, lo], axis=1), tri_ref[...], preferred_element_type=F32))
        for h in heads:
            vb = v_ref[0, pl.ds(start, LANES), cols[h]]
            if diag:
                w = jnp.where(strict, jnp.exp2(zs[h] - sps[h] - css[h][:, :LANES]), 0.0)
                acc_ref[:, cols[h]] = jnp.dot(w.astype(BF16), vb, preferred_element_type=F32)
                c_ref[:, cols[h]] = css[h][:, LANES:]
            else:
                w = jnp.exp2(zs[h] - sps[h] - (css[h][:, :LANES] + c_ref[:, cols[h]]))
                acc_ref[:, cols[h]] += jnp.dot(w.astype(BF16), vb, preferred_element_type=F32)
                c_ref[:, cols[h]] += css[h][:, LANES:]

    block(pl.multiple_of(i * LANES, LANES), True)

    def body(t, carry):
        block(pl.multiple_of((i - 1 - t) * LANES, LANES), False)
        return carry

    lax.fori_loop(0, i, body, 0)
    o_ref[0] = acc_ref[...].astype(o_ref.dtype)


def _cumsum_matrix():
    j = lax.broadcasted_iota(jnp.int32, (2 * LANES, 2 * LANES), 0) % LANES
    s = lax.broadcasted_iota(jnp.int32, (2 * LANES, 2 * LANES), 1)
    return jnp.where((s >= LANES) | (j > s), 1.0, 0.0).astype(BF16)


def _attn(proj3):
    b, s, _ = proj3.shape
    w = SB_HEADS * LANES
    return pl.pallas_call(
        functools.partial(_attn_kernel, scale=LANES ** -0.5),
        grid=(b, s // LANES),
        in_specs=[
            pl.BlockSpec((1, LANES, w), lambda bi, i: (bi, i, 0)),
            pl.BlockSpec((1, s, w), lambda bi, i: (bi, 0, 1)),
            pl.BlockSpec((1, s, w), lambda bi, i: (bi, 0, 2)),
            pl.BlockSpec((2 * LANES, 2 * LANES), lambda bi, i: (0, 0)),
        ],
        out_specs=pl.BlockSpec((1, LANES, w), lambda bi, i: (bi, i, 0)),
        out_shape=jax.ShapeDtypeStruct((b, s, w), BF16),
        scratch_shapes=[pltpu.VMEM((LANES, w), F32), pltpu.VMEM((LANES, w), F32)],
        compiler_params=pltpu.CompilerParams(
            dimension_semantics=("parallel", "arbitrary"), vmem_limit_bytes=VMEM_LIMIT),
        name="attn",
    )(proj3, proj3, proj3, _cumsum_matrix())


def _mix_out_kernel(ysb_ref, u_ref, vb_ref, sgw_ref, sgb_ref, x_ref, wout_ref, gsb_ref, gsg_ref, nf_ref,
                    x1_ref, h2_ref, ysg_ref):
    tm = x_ref.shape[0]
    p = lax.broadcasted_iota(jnp.int32, (LANES, LANES), 0) // CHUNK
    q = lax.broadcasted_iota(jnp.int32, (LANES, LANES), 1) // CHUNK
    causal = p >= q
    for g in range(SG_GROUPS):
        w = jnp.where(causal, sgw_ref[g], 0.0).astype(BF16)
        cols = slice(g * LANES, (g + 1) * LANES)
        for r in range(tm // LANES):
            rows = slice(r * LANES, (r + 1) * LANES)
            mixed = jnp.dot(w, vb_ref[rows, cols], preferred_element_type=F32) + sgb_ref[g]
            ysg_ref[rows, cols] = u_ref[rows, cols].astype(F32) * mixed
    sbw = ysb_ref.shape[1]
    y_sb = (_rms(ysb_ref[...].astype(F32)) * gsb_ref[...]).astype(BF16)
    y_sg = (_rms(ysg_ref[...]) * gsg_ref[...]).astype(BF16)
    y = jnp.dot(y_sb, wout_ref[:sbw, :], preferred_element_type=F32)
    y = y + jnp.dot(y_sg, wout_ref[sbw:, :], preferred_element_type=F32)
    x1 = x_ref[...] + y
    x1_ref[...] = x1
    h2_ref[...] = (_rms(x1) * nf_ref[...]).astype(BF16)


def _mix_out(y_sb, proj, sg_w, sg_b_bcast, x2, w_out, g_sb, g_sg, norm_ffn, *, tm=256):
    t, d = x2.shape
    sbw = y_sb.shape[1]
    sgw = SG_GROUPS * LANES
    u_blk = (3 * sbw) // sgw
    return pl.pallas_call(
        _mix_out_kernel,
        grid=(t // tm,),
        in_specs=[
            pl.BlockSpec((tm, sbw), lambda i: (i, 0)),
            pl.BlockSpec((tm, sgw), lambda i: (i, u_blk)),
            pl.BlockSpec((tm, sgw), lambda i: (i, u_blk + 1)),
            pl.BlockSpec((SG_GROUPS, LANES, LANES), lambda i: (0, 0, 0)),
            pl.BlockSpec((SG_GROUPS, LANES, LANES), lambda i: (0, 0, 0)),
            pl.BlockSpec((tm, d), lambda i: (i, 0)),
            pl.BlockSpec((sbw + sgw, d), lambda i: (0, 0)),
            pl.BlockSpec((1, sbw), lambda i: (0, 0)),
            pl.BlockSpec((1, sgw), lambda i: (0, 0)),
            pl.BlockSpec((1, d), lambda i: (0, 0)),
        ],
        out_specs=[pl.BlockSpec((tm, d), lambda i: (i, 0)), pl.BlockSpec((tm, d), lambda i: (i, 0))],
        out_shape=[jax.ShapeDtypeStruct((t, d), F32), jax.ShapeDtypeStruct((t, d), BF16)],
        scratch_shapes=[pltpu.VMEM((tm, sgw), F32)],
        compiler_params=pltpu.CompilerParams(
            dimension_semantics=("parallel",), vmem_limit_bytes=VMEM_LIMIT),
        name="mix_out",
    )(y_sb, proj, proj, sg_w, sg_b_bcast, x2, w_out, g_sb, g_sg, norm_ffn)


def _query_kernel(h2_ref, wq_ref, keys_ref, s2_ref, st_ref):
    qf = jnp.dot(h2_ref[...], wq_ref[...], preferred_element_type=F32)
    hw = 2 * PEER_KEYS
    for h in range(PEER_HEADS):
        qn = _rms(qf[:, h * hw:(h + 1) * hw]).astype(BF16)
        for half in range(2):
            qk = qn[:, half * PEER_KEYS:(half + 1) * PEER_KEYS]
            s = lax.dot_general(keys_ref[half], qk, _NT, preferred_element_type=F32)
            if half == 1:
                s2_ref[h] = s
            for c in range(s.shape[1] // LANES):
                st_ref[h, half, :, c, :] = s[:, c * LANES:(c + 1) * LANES]


def _query(h2, w_query, sub_keys, *, tm=SUBLANES * LANES):
    t, d = h2.shape
    return pl.pallas_call(
        _query_kernel,
        grid=(t // tm,),
        in_specs=[
            pl.BlockSpec((tm, d), lambda i: (i, 0)),
            pl.BlockSpec(w_query.shape, lambda i: (0, 0), pipeline_mode=pl.Buffered(1)),
            pl.BlockSpec(sub_keys.shape, lambda i: (0, 0, 0)),
        ],
        out_specs=[
            pl.BlockSpec((PEER_HEADS, PEER_KEYS, tm), lambda i: (0, 0, i)),
            pl.BlockSpec((PEER_HEADS, 2, PEER_KEYS, tm // LANES, LANES), lambda i: (0, 0, 0, i, 0)),
        ],
        out_shape=[
            jax.ShapeDtypeStruct((PEER_HEADS, PEER_KEYS, t), F32),
            jax.ShapeDtypeStruct((PEER_HEADS, 2, PEER_KEYS, t // LANES, LANES), F32),
        ],
        compiler_params=pltpu.CompilerParams(
            dimension_semantics=("parallel",), vmem_limit_bytes=VMEM_LIMIT),
        name="query",
    )(h2, w_query, sub_keys)


def _sort_pairs(n):
    pairs = []
    p = 1
    while p < n:
        k = p
        while k >= 1:
            for j in range(k % p, n - k, 2 * k):
                for i in range(min(k, n - j - k)):
                    if (i + j) // (2 * p) == (i + j + k) // (2 * p):
                        pairs.append((i + j, i + j + k))
            k //= 2
        p *= 2
    return pairs


def _exchange(v, lo, hi):
    a, b = v[lo], v[hi]
    if b is None:
        return
    if a is None:
        v[lo], v[hi] = b, None
        return
    v[lo], v[hi] = jnp.maximum(a, b), jnp.minimum(a, b)


def _sort_desc(v):
    v = list(v)
    for lo, hi in _sort_pairs(len(v)):
        _exchange(v, lo, hi)
    return v


def _merge_top(x, y):
    n = len(x)
    v = [jnp.maximum(x[i], y[n - 1 - i]) for i in range(n)]
    k = n // 2
    while k >= 1:
        for i in range(n):
            if i & k == 0:
                _exchange(v, i, i + k)
        k //= 2
    return v


def _top_desc(vals, k):
    groups = [_sort_desc(vals[g:g + k]) for g in range(0, len(vals), k)]
    while len(groups) > 1:
        groups = [_merge_top(groups[g], groups[g + 1]) for g in range(0, len(groups), 2)]
    return groups[0]


def _topk_kernel(s_ref, th_ref, e1_ref, m2_ref):
    k = PEER_TOPK
    a = [s_ref[0, 0, n] for n in range(PEER_KEYS)]
    b = [s_ref[0, 1, n] for n in range(PEER_KEYS)]
    ta = _top_desc(a, k)
    tb = _top_desc(b, k)
    cand = [ta[p] + tb[q] for p in range(k) for q in range(k) if (p + 1) * (q + 1) <= k]
    n_sort = 1 << (len(cand) - 1).bit_length()
    tau = _sort_desc(cand + [None] * (n_sort - len(cand)))[k - 1]
    top = ta[0] + tb[0]
    z = jnp.zeros_like(top)
    for c in cand:
        z = z + jnp.where(c >= tau, jnp.exp(c - top), 0.0)
    half_inv_z = 0.5 / z
    m2_ref[0] = tb[0]
    for n in range(PEER_KEYS):
        th = jnp.full_like(top, jnp.inf)
        for q in range(k):
            th = jnp.where(a[n] + tb[q] >= tau, tb[q], th)
        th_ref[0, n] = th
        e1_ref[0, n] = jnp.exp(a[n] - ta[0]) * half_inv_z


def _topk(st5):
    hh, _, nk, nc, _ = st5.shape
    blk = (1, nk, SUBLANES, LANES)
    return pl.pallas_call(
        _topk_kernel,
        grid=(hh, nc // SUBLANES),
        in_specs=[pl.BlockSpec((1, 2, nk, SUBLANES, LANES), lambda h, u: (h, 0, 0, u, 0))],
        out_specs=[
            pl.BlockSpec(blk, lambda h, u: (h, 0, u, 0)),
            pl.BlockSpec(blk, lambda h, u: (h, 0, u, 0)),
            pl.BlockSpec((1, SUBLANES, LANES), lambda h, u: (h, u, 0)),
        ],
        out_shape=[
            jax.ShapeDtypeStruct((hh, nk, nc, LANES), F32),
            jax.ShapeDtypeStruct((hh, nk, nc, LANES), F32),
            jax.ShapeDtypeStruct((hh, nc, LANES), F32),
        ],
        compiler_params=pltpu.CompilerParams(
            dimension_semantics=("parallel", "parallel"), vmem_limit_bytes=VMEM_LIMIT),
        name="topk",
    )(st5)


_PACK_ROWS = 16


def _peer_kernel(h2_ref, u_ref, vt_ref, s2_ref, m2_ref, th_ref, e1_ref, o_ref,
                 st0_ref, st1_ref, w0_ref, w1_ref, e2_ref, *, nj, n_tiles):
    s = pl.program_id(0)
    st_ref = (st0_ref, st1_ref)
    w_ref = (w0_ref, w1_ref)
    te, tm = w0_ref.shape
    n_chunks = tm // LANES
    n_l = te // PEER_KEYS
    heads = range(PEER_HEADS)
    sb = jnp.clip(s - 1, 0, n_tiles - 1)
    sc = jnp.clip(s - 2, 0, n_tiles - 1)
    ib, jb = sb // nj, sb % nj
    chunk0 = (ib % (SUBLANES // n_chunks)) * n_chunks

    @pl.when(s == 0)
    def _():
        st1_ref[...] = jnp.zeros_like(st1_ref)
        w0_ref[...] = jnp.zeros_like(w0_ref)

    @pl.when(jb == 0)
    def _():
        for h in heads:
            for c in range(n_chunks):
                lanes = slice(c * LANES, (c + 1) * LANES)
                e2_ref[h, :, lanes] = jnp.exp(s2_ref[h, :, lanes] - m2_ref[h, pl.ds(chunk0 + c, 1), :])

    @pl.when(sc % nj == 0)
    def _():
        o_ref[...] = jnp.zeros_like(o_ref)

    d = o_ref.shape[0]
    halves = (0, 1)

    def step(par):
        def scores(mh, nt):
            m = slice(mh * (te // 2), (mh + 1) * (te // 2))
            n = slice(nt * (tm // 2), (nt + 1) * (tm // 2))
            st_ref[par][m, n] = lax.dot_general(u_ref[m, :], h2_ref[n, :], _NT, preferred_element_type=F32)

        def accumulate(mh, nt):
            m = slice(mh * (d // 2), (mh + 1) * (d // 2))
            n = slice(nt * (tm // 2), (nt + 1) * (tm // 2))
            o_ref[m, n] += jnp.dot(vt_ref[m, :], w_ref[par][:, n], preferred_element_type=F32)

        def gate_rows(c, r):
            lanes = slice(c * LANES, (c + 1) * LANES)
            keys = slice(r * _PACK_ROWS, (r + 1) * _PACK_ROWS)
            gates = [None] * n_l
            for h in heads:
                s2 = s2_ref[h, keys, lanes]
                e2 = e2_ref[h, keys, lanes]
                for l in range(n_l):
                    i1 = jb * n_l + l
                    th = th_ref[h, i1, pl.ds(chunk0 + c, 1), :]
                    e1 = e1_ref[h, i1, pl.ds(chunk0 + c, 1), :]
                    term = jnp.where(s2 >= th, e2, 0.0) * e1
                    gates[l] = term if gates[l] is None else gates[l] + term
            for l in range(n_l):
                rows = slice(l * PEER_KEYS + r * _PACK_ROWS, l * PEER_KEYS + (r + 1) * _PACK_ROWS)
                a = st_ref[1 - par][rows, lanes]
                w_ref[1 - par][rows, lanes] = (
                    a * (1.0 + lax.erf(a * (1.0 / math.sqrt(2.0)))) * gates[l]).astype(BF16)

        mxu_work = []
        for nt in halves:
            for mh in halves:
                mxu_work += [functools.partial(scores, mh, nt), functools.partial(accumulate, mh, nt)]
        vpu_work = [functools.partial(gate_rows, c, r) for r in range(PEER_KEYS // _PACK_ROWS)
                    for c in range(n_chunks)]
        per = len(vpu_work) // len(mxu_work)
        for k, mm in enumerate(mxu_work):
            mm()
            for blk in vpu_work[k * per:(k + 1) * per]:
                blk()

    @pl.when(s % 2 == 0)
    def _():
        step(0)

    @pl.when(s % 2 == 1)
    def _():
        step(1)


def _peer(h2, u_bf, vt_bf, s2, m2, th, e1, *, tm=512, te=512):
    t, d = h2.shape
    n_exp = u_bf.shape[0]
    nk = PEER_KEYS
    nj = n_exp // te
    n_tiles = (t // tm) * nj
    tile_a = lambda s: jnp.minimum(s, n_tiles - 1)
    tile_b = lambda s: jnp.clip(s - 1, 0, n_tiles - 1)
    tile_c = lambda s: jnp.clip(s - 2, 0, n_tiles - 1)
    big = lambda s: (tile_b(s) // nj) * tm // (SUBLANES * LANES)
    return pl.pallas_call(
        functools.partial(_peer_kernel, nj=nj, n_tiles=n_tiles),
        grid=(n_tiles + 2,),
        in_specs=[
            pl.BlockSpec((tm, d), lambda s: (tile_a(s) // nj, 0)),
            pl.BlockSpec((te, d), lambda s: (tile_a(s) % nj, 0)),
            pl.BlockSpec((d, te), lambda s: (0, tile_c(s) % nj)),
            pl.BlockSpec((PEER_HEADS, nk, tm), lambda s: (0, 0, tile_b(s) // nj)),
            pl.BlockSpec((PEER_HEADS, SUBLANES, LANES), lambda s: (0, big(s), 0)),
            pl.BlockSpec((PEER_HEADS, nk, SUBLANES, LANES), lambda s: (0, 0, big(s), 0)),
            pl.BlockSpec((PEER_HEADS, nk, SUBLANES, LANES), lambda s: (0, 0, big(s), 0)),
        ],
        out_specs=pl.BlockSpec((d, tm), lambda s: (0, tile_c(s) // nj)),
        out_shape=jax.ShapeDtypeStruct((d, t), F32),
        scratch_shapes=[pltpu.VMEM((te, tm), F32), pltpu.VMEM((te, tm), F32),
                        pltpu.VMEM((te, tm), BF16), pltpu.VMEM((te, tm), BF16),
                        pltpu.VMEM((PEER_HEADS, nk, tm), F32)],
        compiler_params=pltpu.CompilerParams(
            dimension_semantics=("arbitrary",), vmem_limit_bytes=VMEM_LIMIT),
        name="peer",
    )(h2, u_bf, vt_bf, s2, m2, th, e1)


def _layer(x, norm_mix, w_in, q_norm, k_norm, sg_norm, sg_w, sg_b, out_norm_sb, out_norm_sg,
           w_out, norm_ffn, peer_w_query, peer_sub_keys, peer_u, peer_v):
    b, s, d = x.shape
    t = b * s
    x2 = x.reshape(t, d)
    row = lambda v: v.reshape(1, -1)

    proj = _inproj(x2, row(norm_mix), w_in.astype(BF16), row(q_norm), row(k_norm), row(sg_norm))
    y_sb = _attn(proj.reshape(b, s, -1)).reshape(t, -1)
    sg_b_bcast = jnp.broadcast_to(sg_b[:, :, None], sg_w.shape)
    x1, h2 = _mix_out(y_sb, proj, sg_w, sg_b_bcast, x2, w_out.astype(BF16),
                      row(out_norm_sb), row(out_norm_sg), row(norm_ffn))

    s2, st5 = _query(h2, peer_w_query.astype(BF16), peer_sub_keys.astype(BF16))
    th, e1, m2 = _topk(st5)
    out_t = _peer(h2, peer_u.astype(BF16), peer_v.astype(BF16).T, s2, m2, th, e1)
    return (x1 + out_t.T).reshape(b, s, d)


def kernel(x, norm_mix, w_in, q_norm, k_norm, sg_norm, sg_w, sg_b, out_norm_sb, out_norm_sg,
           w_out, norm_ffn, peer_w_query, peer_sub_keys, peer_u, peer_v):
    for layer in range(norm_mix.shape[0]):
        x = _layer(x, norm_mix[layer], w_in[layer], q_norm[layer], k_norm[layer], sg_norm[layer],
                   sg_w[layer], sg_b[layer], out_norm_sb[layer], out_norm_sg[layer], w_out[layer],
                   norm_ffn[layer], peer_w_query[layer], peer_sub_keys[layer], peer_u[layer],
                   peer_v[layer])
    return x
```

```python
import functools
import math

import jax
import jax.numpy as jnp
from jax import lax
from jax.experimental import pallas as pl
from jax.experimental.pallas import tpu as pltpu

F32 = jnp.float32
BF16 = jnp.bfloat16

RMS_EPS = 1e-6
LANES = 128
SUBLANES = 8
CHUNK = 64
SB_HEADS = 8
SG_GROUPS = 8
PEER_HEADS = 8
PEER_KEYS = 128
PEER_TOPK = 16
VMEM_LIMIT = 56 * 1024 * 1024

_NT = (((1,), (1,)), ((), ()))
_LOG2E = 1.0 / math.log(2.0)


def _gelu(x):
    return 0.5 * x * (1.0 + lax.erf(x * (1.0 / math.sqrt(2.0))))


def _rms(x):
    return x * lax.rsqrt(jnp.mean(x * x, axis=-1, keepdims=True) + RMS_EPS)


def _inproj_kernel(x_ref, nm_ref, w_ref, qn_ref, kn_ref, sgn_ref, o_ref, h_ref, *, tiles_per_region):
    j = pl.program_id(1)
    tn = o_ref.shape[1]

    @pl.when(j == 0)
    def _():
        h_ref[...] = (_rms(x_ref[...]) * nm_ref[...]).astype(BF16)

    region = j // tiles_per_region
    mxu_n = 2 * LANES

    def sweep(epilogue):
        for c0 in range(0, tn, mxu_n):
            acc = jnp.dot(h_ref[...], w_ref[:, c0:c0 + mxu_n], preferred_element_type=F32)
            for c in range(c0, c0 + mxu_n, LANES):
                sl = slice(c, c + LANES)
                o_ref[:, sl] = epilogue(acc[:, c - c0:c - c0 + LANES], sl).astype(o_ref.dtype)

    @pl.when(region == 0)
    def _():
        sweep(lambda a, sl: _rms(a) * qn_ref[...])

    @pl.when(region == 1)
    def _():
        sweep(lambda a, sl: _rms(a) * kn_ref[...])

    @pl.when(region == 2)
    def _():
        sweep(lambda a, sl: a)

    @pl.when(region == 3)
    def _():
        sweep(lambda a, sl: _gelu(a))

    @pl.when(region == 4)
    def _():
        sweep(lambda a, sl: _rms(_gelu(a)) * sgn_ref[:, sl])


def _inproj(x2, norm_mix, w_in, q_norm, k_norm, sg_norm, *, tm=512, tn=1024):
    t, d = x2.shape
    n = w_in.shape[1]
    region_w = SB_HEADS * LANES
    tpr = region_w // tn
    return pl.pallas_call(
        functools.partial(_inproj_kernel, tiles_per_region=tpr),
        grid=(t // tm, n // tn),
        in_specs=[
            pl.BlockSpec((tm, d), lambda i, j: (i, 0)),
            pl.BlockSpec((1, d), lambda i, j: (0, 0)),
            pl.BlockSpec((d, tn), lambda i, j: (0, j)),
            pl.BlockSpec((1, LANES), lambda i, j: (0, 0)),
            pl.BlockSpec((1, LANES), lambda i, j: (0, 0)),
            pl.BlockSpec((1, tn), lambda i, j: (0, jnp.maximum(j - 4 * tpr, 0))),
        ],
        out_specs=pl.BlockSpec((tm, tn), lambda i, j: (i, j)),
        out_shape=jax.ShapeDtypeStruct((t, n), BF16),
        scratch_shapes=[pltpu.VMEM((tm, d), BF16)],
        compiler_params=pltpu.CompilerParams(
            dimension_semantics=("parallel", "arbitrary"), vmem_limit_bytes=VMEM_LIMIT),
        name="inproj",
    )(x2, norm_mix, w_in, q_norm, k_norm, sg_norm)


def _attn_kernel(q_ref, k_ref, v_ref, tri_ref, o_ref, acc_ref, c_ref, *, scale):
    i = pl.program_id(1)
    row = lax.broadcasted_iota(jnp.int32, (LANES, LANES), 0)
    col = lax.broadcasted_iota(jnp.int32, (LANES, LANES), 1)
    strict = col < row

    heads = range(SB_HEADS)
    cols = [slice(h * LANES, (h + 1) * LANES) for h in heads]

    def block(start, diag):
        zs = [lax.dot_general(q_ref[0, :, cols[h]], k_ref[0, pl.ds(start, LANES), cols[h]], _NT,
                              preferred_element_type=F32) * (scale * _LOG2E) for h in heads]
        sps = [jnp.maximum(z, 0.0) + jnp.log2(1.0 + jnp.exp2(-jnp.abs(z))) for z in zs]
        if diag:
            sps = [jnp.where(strict, sp, 0.0) for sp in sps]
        css = []
        for sp in sps:
            hi = sp.astype(BF16)
            lo = (sp - hi.astype(F32)).astype(BF16)
            css.append(jnp.dot(jnp.concatenate([hi, lo], axis=1), tri_ref[...], preferred_element_type=F32))
        for h in heads:
            vb = v_ref[0, pl.ds(start, LANES), cols[h]]
            if diag:
                w = jnp.where(strict, jnp.exp2(zs[h] - sps[h] - css[h][:, :LANES]), 0.0)
                acc_ref[:, cols[h]] = jnp.dot(w.astype(BF16), vb, preferred_element_type=F32)
                c_ref[:, cols[h]] = css[h][:, LANES:]
            else:
                w = jnp.exp2(zs[h] - sps[h] - (css[h][:, :LANES] + c_ref[:, cols[h]]))
                acc_ref[:, cols[h]] += jnp.dot(w.astype(BF16), vb, preferred_element_type=F32)
                c_ref[:, cols[h]] += css[h][:, LANES:]

    block(pl.multiple_of(i * LANES, LANES), True)

    def body(t, carry):
        block(pl.multiple_of((i - 1 - t) * LANES, LANES), False)
        return carry

    lax.fori_loop(0, i, body, 0)
    o_ref[0] = acc_ref[...].astype(o_ref.dtype)


def _cumsum_matrix():
    j = lax.broadcasted_iota(jnp.int32, (2 * LANES, 2 * LANES), 0) % LANES
    s = lax.broadcasted_iota(jnp.int32, (2 * LANES, 2 * LANES), 1)
    return jnp.where((s >= LANES) | (j > s), 1.0, 0.0).astype(BF16)


def _attn(proj3):
    b, s, _ = proj3.shape
    w = SB_HEADS * LANES
    return pl.pallas_call(
        functools.partial(_attn_kernel, scale=LANES ** -0.5),
        grid=(b, s // LANES),
        in_specs=[
            pl.BlockSpec((1, LANES, w), lambda bi, i: (bi, i, 0)),
            pl.BlockSpec((1, s, w), lambda bi, i: (bi, 0, 1)),
            pl.BlockSpec((1, s, w), lambda bi, i: (bi, 0, 2)),
            pl.BlockSpec((2 * LANES, 2 * LANES), lambda bi, i: (0, 0)),
        ],
        out_specs=pl.BlockSpec((1, LANES, w), lambda bi, i: (bi, i, 0)),
        out_shape=jax.ShapeDtypeStruct((b, s, w), BF16),
        scratch_shapes=[pltpu.VMEM((LANES, w), F32), pltpu.VMEM((LANES, w), F32)],
        compiler_params=pltpu.CompilerParams(
            dimension_semantics=("parallel", "arbitrary"), vmem_limit_bytes=VMEM_LIMIT),
        name="attn",
    )(proj3, proj3, proj3, _cumsum_matrix())


def _mix_out_kernel(ysb_ref, u_ref, vb_ref, sgw_ref, sgb_ref, x_ref, wout_ref, gsb_ref, gsg_ref, nf_ref,
                    x1_ref, h2_ref, ysg_ref):
    tm = x_ref.shape[0]
    p = lax.broadcasted_iota(jnp.int32, (LANES, LANES), 0) // CHUNK
    q = lax.broadcasted_iota(jnp.int32, (LANES, LANES), 1) // CHUNK
    causal = p >= q
    for g in range(SG_GROUPS):
        w = jnp.where(causal, sgw_ref[g], 0.0).astype(BF16)
        cols = slice(g * LANES, (g + 1) * LANES)
        for r in range(tm // LANES):
            rows = slice(r * LANES, (r + 1) * LANES)
            mixed = jnp.dot(w, vb_ref[rows, cols], preferred_element_type=F32) + sgb_ref[g]
            ysg_ref[rows, cols] = u_ref[rows, cols].astype(F32) * mixed
    sbw = ysb_ref.shape[1]
    y_sb = (_rms(ysb_ref[...].astype(F32)) * gsb_ref[...]).astype(BF16)
    y_sg = (_rms(ysg_ref[...]) * gsg_ref[...]).astype(BF16)
    y = jnp.dot(y_sb, wout_ref[:sbw, :], preferred_element_type=F32)
    y = y + jnp.dot(y_sg, wout_ref[sbw:, :], preferred_element_type=F32)
    x1 = x_ref[...] + y
    x1_ref[...] = x1
    h2_ref[...] = (_rms(x1) * nf_ref[...]).astype(BF16)


def _mix_out(y_sb, proj, sg_w, sg_b_bcast, x2, w_out, g_sb, g_sg, norm_ffn, *, tm=256):
    t, d = x2.shape
    sbw = y_sb.shape[1]
    sgw = SG_GROUPS * LANES
    u_blk = (3 * sbw) // sgw
    return pl.pallas_call(
        _mix_out_kernel,
        grid=(t // tm,),
        in_specs=[
            pl.BlockSpec((tm, sbw), lambda i: (i, 0)),
            pl.BlockSpec((tm, sgw), lambda i: (i, u_blk)),
            pl.BlockSpec((tm, sgw), lambda i: (i, u_blk + 1)),
            pl.BlockSpec((SG_GROUPS, LANES, LANES), lambda i: (0, 0, 0)),
            pl.BlockSpec((SG_GROUPS, LANES, LANES), lambda i: (0, 0, 0)),
            pl.BlockSpec((tm, d), lambda i: (i, 0)),
            pl.BlockSpec((sbw + sgw, d), lambda i: (0, 0)),
            pl.BlockSpec((1, sbw), lambda i: (0, 0)),
            pl.BlockSpec((1, sgw), lambda i: (0, 0)),
            pl.BlockSpec((1, d), lambda i: (0, 0)),
        ],
        out_specs=[pl.BlockSpec((tm, d), lambda i: (i, 0)), pl.BlockSpec((tm, d), lambda i: (i, 0))],
        out_shape=[jax.ShapeDtypeStruct((t, d), F32), jax.ShapeDtypeStruct((t, d), BF16)],
        scratch_shapes=[pltpu.VMEM((tm, sgw), F32)],
        compiler_params=pltpu.CompilerParams(
            dimension_semantics=("parallel",), vmem_limit_bytes=VMEM_LIMIT),
        name="mix_out",
    )(y_sb, proj, proj, sg_w, sg_b_bcast, x2, w_out, g_sb, g_sg, norm_ffn)


def _query_kernel(h2_ref, wq_ref, keys_ref, s2_ref, st_ref):
    qf = jnp.dot(h2_ref[...], wq_ref[...], preferred_element_type=F32)
    hw = 2 * PEER_KEYS
    for h in range(PEER_HEADS):
        qn = _rms(qf[:, h * hw:(h + 1) * hw]).astype(BF16)
        for half in range(2):
            qk = qn[:, half * PEER_KEYS:(half + 1) * PEER_KEYS]
            s = lax.dot_general(keys_ref[half], qk, _NT, preferred_element_type=F32)
            if half == 1:
                s2_ref[h] = s
            for c in range(s.shape[1] // LANES):
                st_ref[h, half, :, c, :] = s[:, c * LANES:(c + 1) * LANES]


def _query(h2, w_query, sub_keys, *, tm=SUBLANES * LANES):
    t, d = h2.shape
    return pl.pallas_call(
        _query_kernel,
        grid=(t // tm,),
        in_specs=[
            pl.BlockSpec((tm, d), lambda i: (i, 0)),
            pl.BlockSpec(w_query.shape, lambda i: (0, 0), pipeline_mode=pl.Buffered(1)),
            pl.BlockSpec(sub_keys.shape, lambda i: (0, 0, 0)),
        ],
        out_specs=[
            pl.BlockSpec((PEER_HEADS, PEER_KEYS, tm), lambda i: (0, 0, i)),
            pl.BlockSpec((PEER_HEADS, 2, PEER_KEYS, tm // LANES, LANES), lambda i: (0, 0, 0, i, 0)),
        ],
        out_shape=[
            jax.ShapeDtypeStruct((PEER_HEADS, PEER_KEYS, t), F32),
            jax.ShapeDtypeStruct((PEER_HEADS, 2, PEER_KEYS, t // LANES, LANES), F32),
        ],
        compiler_params=pltpu.CompilerParams(
            dimension_semantics=("parallel",), vmem_limit_bytes=VMEM_LIMIT),
        name="query",
    )(h2, w_query, sub_keys)


def _sort_pairs(n):
    pairs = []
    p = 1
    while p < n:
        k = p
        while k >= 1:
            for j in range(k % p, n - k, 2 * k):
                for i in range(min(k, n - j - k)):
                    if (i + j) // (2 * p) == (i + j + k) // (2 * p):
                        pairs.append((i + j, i + j + k))
            k //= 2
        p *= 2
    return pairs


def _exchange(v, lo, hi):
    a, b = v[lo], v[hi]
    if b is None:
        return
    if a is None:
        v[lo], v[hi] = b, None
        return
    v[lo], v[hi] = jnp.maximum(a, b), jnp.minimum(a, b)


def _sort_desc(v):
    v = list(v)
    for lo, hi in _sort_pairs(len(v)):
        _exchange(v, lo, hi)
    return v


def _merge_top(x, y):
    n = len(x)
    v = [jnp.maximum(x[i], y[n - 1 - i]) for i in range(n)]
    k = n // 2
    while k >= 1:
        for i in range(n):
            if i & k == 0:
                _exchange(v, i, i + k)
        k //= 2
    return v


def _top_desc(vals, k):
    groups = [_sort_desc(vals[g:g + k]) for g in range(0, len(vals), k)]
    while len(groups) > 1:
        groups = [_merge_top(groups[g], groups[g + 1]) for g in range(0, len(groups), 2)]
    return groups[0]


def _topk_kernel(s_ref, th_ref, e1_ref, m2_ref):
    k = PEER_TOPK
    a = [s_ref[0, 0, n] for n in range(PEER_KEYS)]
    b = [s_ref[0, 1, n] for n in range(PEER_KEYS)]
    ta = _top_desc(a, k)
    tb = _top_desc(b, k)
    cand = [ta[p] + tb[q] for p in range(k) for q in range(k) if (p + 1) * (q + 1) <= k]
    n_sort = 1 << (len(cand) - 1).bit_length()
    tau = _sort_desc(cand + [None] * (n_sort - len(cand)))[k - 1]
    top = ta[0] + tb[0]
    z = jnp.zeros_like(top)
    for c in cand:
        z = z + jnp.where(c >= tau, jnp.exp(c - top), 0.0)
    half_inv_z = 0.5 / z
    m2_ref[0] = tb[0]
    for n in range(PEER_KEYS):
        th = jnp.full_like(top, jnp.inf)
        for q in range(k):
            th = jnp.where(a[n] + tb[q] >= tau, tb[q], th)
        th_ref[0, n] = th
        e1_ref[0, n] = jnp.exp(a[n] - ta[0]) * half_inv_z


def _topk(st5):
    hh, _, nk, nc, _ = st5.shape
    blk = (1, nk, SUBLANES, LANES)
    return pl.pallas_call(
        _topk_kernel,
        grid=(hh, nc // SUBLANES),
        in_specs=[pl.BlockSpec((1, 2, nk, SUBLANES, LANES), lambda h, u: (h, 0, 0, u, 0))],
        out_specs=[
            pl.BlockSpec(blk, lambda h, u: (h, 0, u, 0)),
            pl.BlockSpec(blk, lambda h, u: (h, 0, u, 0)),
            pl.BlockSpec((1, SUBLANES, LANES), lambda h, u: (h, u, 0)),
        ],
        out_shape=[
            jax.ShapeDtypeStruct((hh, nk, nc, LANES), F32),
            jax.ShapeDtypeStruct((hh, nk, nc, LANES), F32),
            jax.ShapeDtypeStruct((hh, nc, LANES), F32),
        ],
        compiler_params=pltpu.CompilerParams(
            dimension_semantics=("parallel", "parallel"), vmem_limit_bytes=VMEM_LIMIT),
        name="topk",
    )(st5)


_PACK_ROWS = 16


def _peer_kernel(h2_ref, u_ref, vt_ref, s2_ref, m2_ref, th_ref, e1_ref, x1_ref, o_ref,
                 st0_ref, st1_ref, w0_ref, w1_ref, e2_ref, acc_ref, *, nj, n_tiles):
    s = pl.program_id(0)
    st_ref = (st0_ref, st1_ref)
    w_ref = (w0_ref, w1_ref)
    te, tm = w0_ref.shape
    n_chunks = tm // LANES
    n_l = te // PEER_KEYS
    heads = range(PEER_HEADS)
    sb = jnp.clip(s - 1, 0, n_tiles - 1)
    sc = jnp.clip(s - 2, 0, n_tiles - 1)
    ib, jb = sb // nj, sb % nj
    chunk0 = (ib % (SUBLANES // n_chunks)) * n_chunks

    @pl.when(s == 0)
    def _():
        st1_ref[...] = jnp.zeros_like(st1_ref)
        w0_ref[...] = jnp.zeros_like(w0_ref)

    @pl.when(jb == 0)
    def _():
        for h in heads:
            for c in range(n_chunks):
                lanes = slice(c * LANES, (c + 1) * LANES)
                e2_ref[h, :, lanes] = jnp.exp(s2_ref[h, :, lanes] - m2_ref[h, pl.ds(chunk0 + c, 1), :])

    @pl.when(sc % nj == 0)
    def _():
        acc_ref[...] = jnp.zeros_like(acc_ref)

    d = acc_ref.shape[0]
    halves = (0, 1)

    def step(par):
        def scores(mh, nt):
            m = slice(mh * (te // 2), (mh + 1) * (te // 2))
            n = slice(nt * (tm // 2), (nt + 1) * (tm // 2))
            st_ref[par][m, n] = lax.dot_general(u_ref[m, :], h2_ref[n, :], _NT, preferred_element_type=F32)

        def accumulate(mh, nt):
            m = slice(mh * (d // 2), (mh + 1) * (d // 2))
            n = slice(nt * (tm // 2), (nt + 1) * (tm // 2))
            acc_ref[m, n] += jnp.dot(vt_ref[m, :], w_ref[par][:, n], preferred_element_type=F32)

        def gate_rows(c, r):
            lanes = slice(c * LANES, (c + 1) * LANES)
            keys = slice(r * _PACK_ROWS, (r + 1) * _PACK_ROWS)
            gates = [None] * n_l
            for h in heads:
                s2 = s2_ref[h, keys, lanes]
                e2 = e2_ref[h, keys, lanes]
                for l in range(n_l):
                    i1 = jb * n_l + l
                    th = th_ref[h, i1, pl.ds(chunk0 + c, 1), :]
                    e1 = e1_ref[h, i1, pl.ds(chunk0 + c, 1), :]
                    term = jnp.where(s2 >= th, e2, 0.0) * e1
                    gates[l] = term if gates[l] is None else gates[l] + term
            for l in range(n_l):
                rows = slice(l * PEER_KEYS + r * _PACK_ROWS, l * PEER_KEYS + (r + 1) * _PACK_ROWS)
                a = st_ref[1 - par][rows, lanes]
                w_ref[1 - par][rows, lanes] = (
                    a * (1.0 + lax.erf(a * (1.0 / math.sqrt(2.0)))) * gates[l]).astype(BF16)

        mxu_work = []
        for nt in halves:
            for mh in halves:
                mxu_work += [functools.partial(scores, mh, nt), functools.partial(accumulate, mh, nt)]
        vpu_work = [functools.partial(gate_rows, c, r) for r in range(PEER_KEYS // _PACK_ROWS)
                    for c in range(n_chunks)]
        per = len(vpu_work) // len(mxu_work)
        for k, mm in enumerate(mxu_work):
            mm()
            for blk in vpu_work[k * per:(k + 1) * per]:
                blk()

    @pl.when(s % 2 == 0)
    def _():
        step(0)

    @pl.when(s % 2 == 1)
    def _():
        step(1)

    @pl.when((s >= 2) & (sc % nj == nj - 1))
    def _():
        for n in range(tm // LANES):
            rows = slice(n * LANES, (n + 1) * LANES)
            for m in range(d // LANES):
                cols = slice(m * LANES, (m + 1) * LANES)
                o_ref[rows, cols] = x1_ref[rows, cols] + acc_ref[cols, rows].T


def _peer(h2, u_bf, vt_bf, s2, m2, th, e1, x1, *, tm=512, te=512):
    t, d = h2.shape
    n_exp = u_bf.shape[0]
    nk = PEER_KEYS
    nj = n_exp // te
    n_tiles = (t // tm) * nj
    tile_a = lambda s: jnp.minimum(s, n_tiles - 1)
    tile_b = lambda s: jnp.clip(s - 1, 0, n_tiles - 1)
    tile_c = lambda s: jnp.clip(s - 2, 0, n_tiles - 1)
    big = lambda s: (tile_b(s) // nj) * tm // (SUBLANES * LANES)
    once = dict(pipeline_mode=pl.Buffered(1))
    return pl.pallas_call(
        functools.partial(_peer_kernel, nj=nj, n_tiles=n_tiles),
        grid=(n_tiles + 2,),
        in_specs=[
            pl.BlockSpec((tm, d), lambda s: (tile_a(s) // nj, 0), **once),
            pl.BlockSpec((te, d), lambda s: (tile_a(s) % nj, 0)),
            pl.BlockSpec((d, te), lambda s: (0, tile_c(s) % nj)),
            pl.BlockSpec((PEER_HEADS, nk, tm), lambda s: (0, 0, tile_b(s) // nj), **once),
            pl.BlockSpec((PEER_HEADS, SUBLANES, LANES), lambda s: (0, big(s), 0)),
            pl.BlockSpec((PEER_HEADS, nk, SUBLANES, LANES), lambda s: (0, 0, big(s), 0), **once),
            pl.BlockSpec((PEER_HEADS, nk, SUBLANES, LANES), lambda s: (0, 0, big(s), 0), **once),
            pl.BlockSpec((tm, d), lambda s: (tile_c(s) // nj, 0), **once),
        ],
        out_specs=pl.BlockSpec((tm, d), lambda s: (tile_c(s) // nj, 0)),
        out_shape=jax.ShapeDtypeStruct((t, d), F32),
        scratch_shapes=[pltpu.VMEM((te, tm), F32), pltpu.VMEM((te, tm), F32),
                        pltpu.VMEM((te, tm), BF16), pltpu.VMEM((te, tm), BF16),
                        pltpu.VMEM((PEER_HEADS, nk, tm), F32), pltpu.VMEM((d, tm), F32)],
        compiler_params=pltpu.CompilerParams(
            dimension_semantics=("arbitrary",), vmem_limit_bytes=VMEM_LIMIT),
        name="peer",
    )(h2, u_bf, vt_bf, s2, m2, th, e1, x1)


def _transpose_cast_kernel(x_ref, o_ref):
    o_ref[...] = x_ref[...].T.astype(o_ref.dtype)


def _transpose_cast(x, dtype, *, blk=512):
    r, c = x.shape
    return pl.pallas_call(
        _transpose_cast_kernel,
        grid=(r // blk, c // blk),
        in_specs=[pl.BlockSpec((blk, blk), lambda i, j: (i, j))],
        out_specs=pl.BlockSpec((blk, blk), lambda i, j: (j, i)),
        out_shape=jax.ShapeDtypeStruct((c, r), dtype),
        compiler_params=pltpu.CompilerParams(
            dimension_semantics=("parallel", "parallel"), vmem_limit_bytes=VMEM_LIMIT),
        name="transpose_cast",
    )(x)


def _layer(x, norm_mix, w_in, q_norm, k_norm, sg_norm, sg_w, sg_b, out_norm_sb, out_norm_sg,
           w_out, norm_ffn, peer_w_query, peer_sub_keys, peer_u, peer_v):
    b, s, d = x.shape
    t = b * s
    x2 = x.reshape(t, d)
    row = lambda v: v.reshape(1, -1)

    proj = _inproj(x2, row(norm_mix), w_in.astype(BF16), row(q_norm), row(k_norm), row(sg_norm))
    y_sb = _attn(proj.reshape(b, s, -1)).reshape(t, -1)
    sg_b_bcast = jnp.broadcast_to(sg_b[:, :, None], sg_w.shape)
    x1, h2 = _mix_out(y_sb, proj, sg_w, sg_b_bcast, x2, w_out.astype(BF16),
                      row(out_norm_sb), row(out_norm_sg), row(norm_ffn))

    s2, st5 = _query(h2, peer_w_query.astype(BF16), peer_sub_keys.astype(BF16))
    th, e1, m2 = _topk(st5)
    out = _peer(h2, peer_u.astype(BF16), _transpose_cast(peer_v, BF16), s2, m2, th, e1, x1)
    return out.reshape(b, s, d)


def kernel(x, norm_mix, w_in, q_norm, k_norm, sg_norm, sg_w, sg_b, out_norm_sb, out_norm_sg,
           w_out, norm_ffn, peer_w_query, peer_sub_keys, peer_u, peer_v):
    for layer in range(norm_mix.shape[0]):
        x = _layer(x, norm_mix[layer], w_in[layer], q_norm[layer], k_norm[layer], sg_norm[layer],
                   sg_w[layer], sg_b[layer], out_norm_sb[layer], out_norm_sg[layer], w_out[layer],
                   norm_ffn[layer], peer_w_query[layer], peer_sub_keys[layer], peer_u[layer],
                   peer_v[layer])
    return x
```

```python
import functools
import math

import jax
import jax.numpy as jnp
from jax import lax
from jax.experimental import pallas as pl
from jax.experimental.pallas import tpu as pltpu

F32 = jnp.float32
BF16 = jnp.bfloat16

RMS_EPS = 1e-6
LANES = 128
SUBLANES = 8
CHUNK = 64
SB_HEADS = 8
SG_GROUPS = 8
PEER_HEADS = 8
PEER_KEYS = 128
PEER_TOPK = 16
VMEM_LIMIT = 56 * 1024 * 1024

_NT = (((1,), (1,)), ((), ()))
_LOG2E = 1.0 / math.log(2.0)


def _gelu(x):
    return 0.5 * x * (1.0 + lax.erf(x * (1.0 / math.sqrt(2.0))))


def _rms(x):
    return x * lax.rsqrt(jnp.mean(x * x, axis=-1, keepdims=True) + RMS_EPS)


def _inproj_kernel(x_ref, nm_ref, w_ref, qn_ref, kn_ref, sgn_ref, o_ref, h_ref, *, tiles_per_region):
    j = pl.program_id(1)
    tn = o_ref.shape[1]

    @pl.when(j == 0)
    def _():
        h_ref[...] = (_rms(x_ref[...]) * nm_ref[...]).astype(BF16)

    region = j // tiles_per_region
    mxu_n = 2 * LANES

    def sweep(epilogue):
        for c0 in range(0, tn, mxu_n):
            acc = jnp.dot(h_ref[...], w_ref[:, c0:c0 + mxu_n], preferred_element_type=F32)
            for c in range(c0, c0 + mxu_n, LANES):
                sl = slice(c, c + LANES)
                o_ref[:, sl] = epilogue(acc[:, c - c0:c - c0 + LANES], sl).astype(o_ref.dtype)

    @pl.when(region == 0)
    def _():
        sweep(lambda a, sl: _rms(a) * qn_ref[...])

    @pl.when(region == 1)
    def _():
        sweep(lambda a, sl: _rms(a) * kn_ref[...])

    @pl.when(region == 2)
    def _():
        sweep(lambda a, sl: a)

    @pl.when(region == 3)
    def _():
        sweep(lambda a, sl: _gelu(a))

    @pl.when(region == 4)
    def _():
        sweep(lambda a, sl: _rms(_gelu(a)) * sgn_ref[:, sl])


def _inproj(x2, norm_mix, w_in, q_norm, k_norm, sg_norm, *, tm=512, tn=1024):
    t, d = x2.shape
    n = w_in.shape[1]
    region_w = SB_HEADS * LANES
    tpr = region_w // tn
    return pl.pallas_call(
        functools.partial(_inproj_kernel, tiles_per_region=tpr),
        grid=(t // tm, n // tn),
        in_specs=[
            pl.BlockSpec((tm, d), lambda i, j: (i, 0)),
            pl.BlockSpec((1, d), lambda i, j: (0, 0)),
            pl.BlockSpec((d, tn), lambda i, j: (0, j)),
            pl.BlockSpec((1, LANES), lambda i, j: (0, 0)),
            pl.BlockSpec((1, LANES), lambda i, j: (0, 0)),
            pl.BlockSpec((1, tn), lambda i, j: (0, jnp.maximum(j - 4 * tpr, 0))),
        ],
        out_specs=pl.BlockSpec((tm, tn), lambda i, j: (i, j)),
        out_shape=jax.ShapeDtypeStruct((t, n), BF16),
        scratch_shapes=[pltpu.VMEM((tm, d), BF16)],
        compiler_params=pltpu.CompilerParams(
            dimension_semantics=("parallel", "arbitrary"), vmem_limit_bytes=VMEM_LIMIT),
        name="inproj",
    )(x2, norm_mix, w_in, q_norm, k_norm, sg_norm)


def _attn_kernel(q_ref, k_ref, v_ref, tri_ref, o_ref, acc_ref, c_ref, *, scale):
    i = pl.program_id(1)
    row = lax.broadcasted_iota(jnp.int32, (LANES, LANES), 0)
    col = lax.broadcasted_iota(jnp.int32, (LANES, LANES), 1)
    strict = col < row

    heads = range(SB_HEADS)
    cols = [slice(h * LANES, (h + 1) * LANES) for h in heads]

    def blocks(starts, diag):
        chains = [(start, h) for start in starts for h in heads]
        zs = [lax.dot_general(q_ref[0, :, cols[h]], k_ref[0, pl.ds(start, LANES), cols[h]], _NT,
                              preferred_element_type=F32) * (scale * _LOG2E) for start, h in chains]
        sps = [jnp.maximum(z, 0.0) + jnp.log2(1.0 + jnp.exp2(-jnp.abs(z))) for z in zs]
        if diag:
            sps = [jnp.where(strict, sp, 0.0) for sp in sps]
        css = []
        for sp in sps:
            hi = sp.astype(BF16)
            lo = (sp - hi.astype(F32)).astype(BF16)
            css.append(jnp.dot(jnp.concatenate([hi, lo], axis=1), tri_ref[...], preferred_element_type=F32))
        carried = [None if diag else c_ref[:, cols[h]] for h in heads]
        acc = [None if diag else acc_ref[:, cols[h]] for h in heads]
        for n, (start, h) in enumerate(chains):
            tail = css[n][:, :LANES]
            if carried[h] is not None:
                tail = tail + carried[h]
            w = jnp.exp2(zs[n] - sps[n] - tail)
            if diag:
                w = jnp.where(strict, w, 0.0)
            pv = jnp.dot(w.astype(BF16), v_ref[0, pl.ds(start, LANES), cols[h]], preferred_element_type=F32)
            acc[h] = pv if acc[h] is None else acc[h] + pv
            total = css[n][:, LANES:]
            carried[h] = total if carried[h] is None else carried[h] + total
        for h in heads:
            acc_ref[:, cols[h]] = acc[h]
            c_ref[:, cols[h]] = carried[h]

    blocks([pl.multiple_of(i * LANES, LANES)], True)

    def body(t, carry):
        right = pl.multiple_of((i - 1 - 2 * t) * LANES, LANES)
        blocks([right, pl.multiple_of(right - LANES, LANES)], False)
        return carry

    lax.fori_loop(0, i // 2, body, 0)

    @pl.when(i % 2 == 1)
    def _():
        blocks([0], False)

    o_ref[0] = acc_ref[...].astype(o_ref.dtype)


def _cumsum_matrix():
    j = lax.broadcasted_iota(jnp.int32, (2 * LANES, 2 * LANES), 0) % LANES
    s = lax.broadcasted_iota(jnp.int32, (2 * LANES, 2 * LANES), 1)
    return jnp.where((s >= LANES) | (j > s), 1.0, 0.0).astype(BF16)


def _attn(proj3):
    b, s, _ = proj3.shape
    w = SB_HEADS * LANES
    return pl.pallas_call(
        functools.partial(_attn_kernel, scale=LANES ** -0.5),
        grid=(b, s // LANES),
        in_specs=[
            pl.BlockSpec((1, LANES, w), lambda bi, i: (bi, i, 0)),
            pl.BlockSpec((1, s, w), lambda bi, i: (bi, 0, 1)),
            pl.BlockSpec((1, s, w), lambda bi, i: (bi, 0, 2)),
            pl.BlockSpec((2 * LANES, 2 * LANES), lambda bi, i: (0, 0)),
        ],
        out_specs=pl.BlockSpec((1, LANES, w), lambda bi, i: (bi, i, 0)),
        out_shape=jax.ShapeDtypeStruct((b, s, w), BF16),
        scratch_shapes=[pltpu.VMEM((LANES, w), F32), pltpu.VMEM((LANES, w), F32)],
        compiler_params=pltpu.CompilerParams(
            dimension_semantics=("parallel", "arbitrary"), vmem_limit_bytes=VMEM_LIMIT),
        name="attn",
    )(proj3, proj3, proj3, _cumsum_matrix())


def _mix_out_kernel(ysb_ref, u_ref, vb_ref, sgw_ref, sgb_ref, x_ref, wout_ref, gsb_ref, gsg_ref, nf_ref,
                    x1_ref, h2_ref, ysg_ref):
    tm = x_ref.shape[0]
    p = lax.broadcasted_iota(jnp.int32, (LANES, LANES), 0) // CHUNK
    q = lax.broadcasted_iota(jnp.int32, (LANES, LANES), 1) // CHUNK
    causal = p >= q
    for g in range(SG_GROUPS):
        w = jnp.where(causal, sgw_ref[g], 0.0).astype(BF16)
        cols = slice(g * LANES, (g + 1) * LANES)
        for r in range(tm // LANES):
            rows = slice(r * LANES, (r + 1) * LANES)
            mixed = jnp.dot(w, vb_ref[rows, cols], preferred_element_type=F32) + sgb_ref[g]
            ysg_ref[rows, cols] = u_ref[rows, cols].astype(F32) * mixed
    sbw = ysb_ref.shape[1]
    y_sb = (_rms(ysb_ref[...].astype(F32)) * gsb_ref[...]).astype(BF16)
    y_sg = (_rms(ysg_ref[...]) * gsg_ref[...]).astype(BF16)
    y = jnp.dot(y_sb, wout_ref[:sbw, :], preferred_element_type=F32)
    y = y + jnp.dot(y_sg, wout_ref[sbw:, :], preferred_element_type=F32)
    x1 = x_ref[...] + y
    x1_ref[...] = x1
    h2_ref[...] = (_rms(x1) * nf_ref[...]).astype(BF16)


def _mix_out(y_sb, proj, sg_w, sg_b_bcast, x2, w_out, g_sb, g_sg, norm_ffn, *, tm=256):
    t, d = x2.shape
    sbw = y_sb.shape[1]
    sgw = SG_GROUPS * LANES
    u_blk = (3 * sbw) // sgw
    return pl.pallas_call(
        _mix_out_kernel,
        grid=(t // tm,),
        in_specs=[
            pl.BlockSpec((tm, sbw), lambda i: (i, 0)),
            pl.BlockSpec((tm, sgw), lambda i: (i, u_blk)),
            pl.BlockSpec((tm, sgw), lambda i: (i, u_blk + 1)),
            pl.BlockSpec((SG_GROUPS, LANES, LANES), lambda i: (0, 0, 0)),
            pl.BlockSpec((SG_GROUPS, LANES, LANES), lambda i: (0, 0, 0)),
            pl.BlockSpec((tm, d), lambda i: (i, 0)),
            pl.BlockSpec((sbw + sgw, d), lambda i: (0, 0)),
            pl.BlockSpec((1, sbw), lambda i: (0, 0)),
            pl.BlockSpec((1, sgw), lambda i: (0, 0)),
            pl.BlockSpec((1, d), lambda i: (0, 0)),
        ],
        out_specs=[pl.BlockSpec((tm, d), lambda i: (i, 0)), pl.BlockSpec((tm, d), lambda i: (i, 0))],
        out_shape=[jax.ShapeDtypeStruct((t, d), F32), jax.ShapeDtypeStruct((t, d), BF16)],
        scratch_shapes=[pltpu.VMEM((tm, sgw), F32)],
        compiler_params=pltpu.CompilerParams(
            dimension_semantics=("parallel",), vmem_limit_bytes=VMEM_LIMIT),
        name="mix_out",
    )(y_sb, proj, proj, sg_w, sg_b_bcast, x2, w_out, g_sb, g_sg, norm_ffn)


def _query_kernel(h2_ref, wq_ref, keys_ref, s2_ref, st_ref):
    qf = jnp.dot(h2_ref[...], wq_ref[...], preferred_element_type=F32)
    hw = 2 * PEER_KEYS
    for h in range(PEER_HEADS):
        qn = _rms(qf[:, h * hw:(h + 1) * hw]).astype(BF16)
        for half in range(2):
            qk = qn[:, half * PEER_KEYS:(half + 1) * PEER_KEYS]
            s = lax.dot_general(keys_ref[half], qk, _NT, preferred_element_type=F32)
            if half == 1:
                s2_ref[h] = s
            for c in range(s.shape[1] // LANES):
                st_ref[h, half, :, c, :] = s[:, c * LANES:(c + 1) * LANES]


def _query(h2, w_query, sub_keys, *, tm=SUBLANES * LANES):
    t, d = h2.shape
    return pl.pallas_call(
        _query_kernel,
        grid=(t // tm,),
        in_specs=[
            pl.BlockSpec((tm, d), lambda i: (i, 0)),
            pl.BlockSpec(w_query.shape, lambda i: (0, 0), pipeline_mode=pl.Buffered(1)),
            pl.BlockSpec(sub_keys.shape, lambda i: (0, 0, 0)),
        ],
        out_specs=[
            pl.BlockSpec((PEER_HEADS, PEER_KEYS, tm), lambda i: (0, 0, i)),
            pl.BlockSpec((PEER_HEADS, 2, PEER_KEYS, tm // LANES, LANES), lambda i: (0, 0, 0, i, 0)),
        ],
        out_shape=[
            jax.ShapeDtypeStruct((PEER_HEADS, PEER_KEYS, t), F32),
            jax.ShapeDtypeStruct((PEER_HEADS, 2, PEER_KEYS, t // LANES, LANES), F32),
        ],
        compiler_params=pltpu.CompilerParams(
            dimension_semantics=("parallel",), vmem_limit_bytes=VMEM_LIMIT),
        name="query",
    )(h2, w_query, sub_keys)


def _sort_pairs(n):
    pairs = []
    p = 1
    while p < n:
        k = p
        while k >= 1:
            for j in range(k % p, n - k, 2 * k):
                for i in range(min(k, n - j - k)):
                    if (i + j) // (2 * p) == (i + j + k) // (2 * p):
                        pairs.append((i + j, i + j + k))
            k //= 2
        p *= 2
    return pairs


def _exchange(v, lo, hi):
    a, b = v[lo], v[hi]
    if b is None:
        return
    if a is None:
        v[lo], v[hi] = b, None
        return
    v[lo], v[hi] = jnp.maximum(a, b), jnp.minimum(a, b)


def _sort_desc(v):
    v = list(v)
    for lo, hi in _sort_pairs(len(v)):
        _exchange(v, lo, hi)
    return v


def _merge_top(x, y):
    n = len(x)
    v = [jnp.maximum(x[i], y[n - 1 - i]) for i in range(n)]
    k = n // 2
    while k >= 1:
        for i in range(n):
            if i & k == 0:
                _exchange(v, i, i + k)
        k //= 2
    return v


def _top_desc(vals, k):
    groups = [_sort_desc(vals[g:g + k]) for g in range(0, len(vals), k)]
    while len(groups) > 1:
        groups = [_merge_top(groups[g], groups[g + 1]) for g in range(0, len(groups), 2)]
    return groups[0]


def _topk_kernel(s_ref, th_ref, e1_ref, m2_ref):
    k = PEER_TOPK
    a = [s_ref[0, 0, n] for n in range(PEER_KEYS)]
    b = [s_ref[0, 1, n] for n in range(PEER_KEYS)]
    ta = _top_desc(a, k)
    tb = _top_desc(b, k)
    cand = [ta[p] + tb[q] for p in range(k) for q in range(k) if (p + 1) * (q + 1) <= k]
    n_sort = 1 << (len(cand) - 1).bit_length()
    tau = _sort_desc(cand + [None] * (n_sort - len(cand)))[k - 1]
    top = ta[0] + tb[0]
    z = jnp.zeros_like(top)
    for c in cand:
        z = z + jnp.where(c >= tau, jnp.exp(c - top), 0.0)
    half_inv_z = 0.5 / z
    m2_ref[0] = tb[0]
    for n in range(PEER_KEYS):
        th = jnp.full_like(top, jnp.inf)
        for q in range(k):
            th = jnp.where(a[n] + tb[q] >= tau, tb[q], th)
        th_ref[0, n] = th
        e1_ref[0, n] = jnp.exp(a[n] - ta[0]) * half_inv_z


def _topk(st5):
    hh, _, nk, nc, _ = st5.shape
    blk = (1, nk, SUBLANES, LANES)
    return pl.pallas_call(
        _topk_kernel,
        grid=(hh, nc // SUBLANES),
        in_specs=[pl.BlockSpec((1, 2, nk, SUBLANES, LANES), lambda h, u: (h, 0, 0, u, 0))],
        out_specs=[
            pl.BlockSpec(blk, lambda h, u: (h, 0, u, 0)),
            pl.BlockSpec(blk, lambda h, u: (h, 0, u, 0)),
            pl.BlockSpec((1, SUBLANES, LANES), lambda h, u: (h, u, 0)),
        ],
        out_shape=[
            jax.ShapeDtypeStruct((hh, nk, nc, LANES), F32),
            jax.ShapeDtypeStruct((hh, nk, nc, LANES), F32),
            jax.ShapeDtypeStruct((hh, nc, LANES), F32),
        ],
        compiler_params=pltpu.CompilerParams(
            dimension_semantics=("parallel", "parallel"), vmem_limit_bytes=VMEM_LIMIT),
        name="topk",
    )(st5)


_PACK_ROWS = 16
PEER_EXPERT_TILE = 512


def _peer_kernel(h2_ref, u_ref, vt_ref, s2_ref, m2_ref, th_ref, e1_ref, x1_ref, o_ref,
                 st0_ref, st1_ref, w0_ref, w1_ref, e2_ref, acc_ref, *, nj, n_tiles):
    s = pl.program_id(0)
    st_ref = (st0_ref, st1_ref)
    w_ref = (w0_ref, w1_ref)
    te, tm = w0_ref.shape
    n_chunks = tm // LANES
    n_l = te // PEER_KEYS
    heads = range(PEER_HEADS)
    sb = jnp.clip(s - 1, 0, n_tiles - 1)
    sc = jnp.clip(s - 2, 0, n_tiles - 1)
    ib, jb = sb // nj, sb % nj
    chunk0 = (ib % (SUBLANES // n_chunks)) * n_chunks

    @pl.when(s == 0)
    def _():
        st1_ref[...] = jnp.zeros_like(st1_ref)
        w0_ref[...] = jnp.zeros_like(w0_ref)

    @pl.when(jb == 0)
    def _():
        for h in heads:
            for c in range(n_chunks):
                lanes = slice(c * LANES, (c + 1) * LANES)
                e2_ref[h, :, lanes] = jnp.exp(s2_ref[h, :, lanes] - m2_ref[h, pl.ds(chunk0 + c, 1), :])

    @pl.when(sc % nj == 0)
    def _():
        acc_ref[...] = jnp.zeros_like(acc_ref)

    d = acc_ref.shape[0]
    halves = (0, 1)

    def step(par):
        def scores(mh, nt):
            m = slice(mh * (te // 2), (mh + 1) * (te // 2))
            n = slice(nt * (tm // 2), (nt + 1) * (tm // 2))
            st_ref[par][m, n] = lax.dot_general(u_ref[m, :], h2_ref[n, :], _NT, preferred_element_type=F32)

        def accumulate(mh, nt):
            m = slice(mh * (d // 2), (mh + 1) * (d // 2))
            n = slice(nt * (tm // 2), (nt + 1) * (tm // 2))
            acc_ref[m, n] += jnp.dot(vt_ref[m, :], w_ref[par][:, n], preferred_element_type=F32)

        def gate_rows(c, r):
            lanes = slice(c * LANES, (c + 1) * LANES)
            keys = slice(r * _PACK_ROWS, (r + 1) * _PACK_ROWS)
            gates = [None] * n_l
            for h in heads:
                s2 = s2_ref[h, keys, lanes]
                e2 = e2_ref[h, keys, lanes]
                for l in range(n_l):
                    i1 = jb * n_l + l
                    th = th_ref[h, i1, pl.ds(chunk0 + c, 1), :]
                    e1 = e1_ref[h, i1, pl.ds(chunk0 + c, 1), :]
                    term = jnp.where(s2 >= th, e2, 0.0) * e1
                    gates[l] = term if gates[l] is None else gates[l] + term
            for l in range(n_l):
                rows = slice(l * PEER_KEYS + r * _PACK_ROWS, l * PEER_KEYS + (r + 1) * _PACK_ROWS)
                a = st_ref[1 - par][rows, lanes]
                w_ref[1 - par][rows, lanes] = (
                    a * (1.0 + lax.erf(a * (1.0 / math.sqrt(2.0)))) * gates[l]).astype(BF16)

        mxu_work = []
        for nt in halves:
            for mh in halves:
                mxu_work += [functools.partial(scores, mh, nt), functools.partial(accumulate, mh, nt)]
        vpu_work = [functools.partial(gate_rows, c, r) for r in range(PEER_KEYS // _PACK_ROWS)
                    for c in range(n_chunks)]
        per = len(vpu_work) // len(mxu_work)
        for k, mm in enumerate(mxu_work):
            mm()
            for blk in vpu_work[k * per:(k + 1) * per]:
                blk()

    @pl.when(s % 2 == 0)
    def _():
        step(0)

    @pl.when(s % 2 == 1)
    def _():
        step(1)

    @pl.when((s >= 2) & (sc % nj == nj - 1))
    def _():
        for n in range(tm // LANES):
            rows = slice(n * LANES, (n + 1) * LANES)
            for m in range(d // LANES):
                cols = slice(m * LANES, (m + 1) * LANES)
                o_ref[rows, cols] = x1_ref[rows, cols] + acc_ref[cols, rows].T


def _peer(h2, u_bf, vt_bf, s2, m2, th, e1, x1, *, tm=512):
    t, d = h2.shape
    nk = PEER_KEYS
    nj, _, te = vt_bf.shape
    n_tiles = (t // tm) * nj
    tile_a = lambda s: jnp.minimum(s, n_tiles - 1)
    tile_b = lambda s: jnp.clip(s - 1, 0, n_tiles - 1)
    tile_c = lambda s: jnp.clip(s - 2, 0, n_tiles - 1)
    big = lambda s: (tile_b(s) // nj) * tm // (SUBLANES * LANES)
    return pl.pallas_call(
        functools.partial(_peer_kernel, nj=nj, n_tiles=n_tiles),
        grid=(n_tiles + 2,),
        in_specs=[
            pl.BlockSpec((tm, d), lambda s: (tile_a(s) // nj, 0)),
            pl.BlockSpec((te, d), lambda s: (tile_a(s) % nj, 0)),
            pl.BlockSpec((None, d, te), lambda s: (tile_c(s) % nj, 0, 0)),
            pl.BlockSpec((PEER_HEADS, nk, tm), lambda s: (0, 0, tile_b(s) // nj)),
            pl.BlockSpec((PEER_HEADS, SUBLANES, LANES), lambda s: (0, big(s), 0)),
            pl.BlockSpec((PEER_HEADS, nk, SUBLANES, LANES), lambda s: (0, 0, big(s), 0)),
            pl.BlockSpec((PEER_HEADS, nk, SUBLANES, LANES), lambda s: (0, 0, big(s), 0)),
            pl.BlockSpec((tm, d), lambda s: (tile_c(s) // nj, 0), pipeline_mode=pl.Buffered(1)),
        ],
        out_specs=pl.BlockSpec((tm, d), lambda s: (tile_c(s) // nj, 0)),
        out_shape=jax.ShapeDtypeStruct((t, d), F32),
        scratch_shapes=[pltpu.VMEM((te, tm), F32), pltpu.VMEM((te, tm), F32),
                        pltpu.VMEM((te, tm), BF16), pltpu.VMEM((te, tm), BF16),
                        pltpu.VMEM((PEER_HEADS, nk, tm), F32), pltpu.VMEM((d, tm), F32)],
        compiler_params=pltpu.CompilerParams(
            dimension_semantics=("arbitrary",), vmem_limit_bytes=VMEM_LIMIT),
        name="peer",
    )(h2, u_bf, vt_bf, s2, m2, th, e1, x1)


def _transpose_tiles_kernel(x_ref, o_ref):
    blk = o_ref.shape[1]
    for c0 in range(0, o_ref.shape[0], blk):
        o_ref[c0:c0 + blk, :] = x_ref[:, c0:c0 + blk].T.astype(o_ref.dtype)


def _transpose_tiles(x, dtype, *, te):
    r, c = x.shape
    return pl.pallas_call(
        _transpose_tiles_kernel,
        grid=(r // te,),
        in_specs=[pl.BlockSpec((te, c), lambda j: (j, 0))],
        out_specs=pl.BlockSpec((None, c, te), lambda j: (j, 0, 0)),
        out_shape=jax.ShapeDtypeStruct((r // te, c, te), dtype),
        compiler_params=pltpu.CompilerParams(
            dimension_semantics=("parallel",), vmem_limit_bytes=VMEM_LIMIT),
        name="transpose_tiles",
    )(x)


def _layer(x, norm_mix, w_in, q_norm, k_norm, sg_norm, sg_w, sg_b, out_norm_sb, out_norm_sg,
           w_out, norm_ffn, peer_w_query, peer_sub_keys, peer_u, peer_v):
    b, s, d = x.shape
    t = b * s
    x2 = x.reshape(t, d)
    row = lambda v: v.reshape(1, -1)

    proj = _inproj(x2, row(norm_mix), w_in.astype(BF16), row(q_norm), row(k_norm), row(sg_norm))
    y_sb = _attn(proj.reshape(b, s, -1)).reshape(t, -1)
    sg_b_bcast = jnp.broadcast_to(sg_b[:, :, None], sg_w.shape)
    x1, h2 = _mix_out(y_sb, proj, sg_w, sg_b_bcast, x2, w_out.astype(BF16),
                      row(out_norm_sb), row(out_norm_sg), row(norm_ffn))

    s2, st5 = _query(h2, peer_w_query.astype(BF16), peer_sub_keys.astype(BF16))
    th, e1, m2 = _topk(st5)
    vt = _transpose_tiles(peer_v, BF16, te=PEER_EXPERT_TILE)
    out = _peer(h2, peer_u.astype(BF16), vt, s2, m2, th, e1, x1)
    return out.reshape(b, s, d)


def kernel(x, norm_mix, w_in, q_norm, k_norm, sg_norm, sg_w, sg_b, out_norm_sb, out_norm_sg,
           w_out, norm_ffn, peer_w_query, peer_sub_keys, peer_u, peer_v):
    for layer in range(norm_mix.shape[0]):
        x = _layer(x, norm_mix[layer], w_in[layer], q_norm[layer], k_norm[layer], sg_norm[layer],
                   sg_w[layer], sg_b[layer], out_norm_sb[layer], out_norm_sg[layer], w_out[layer],
                   norm_ffn[layer], peer_w_query[layer], peer_sub_keys[layer], peer_u[layer],
                   peer_v[layer])
    return x
```

```python
import functools
import math

import jax
import jax.numpy as jnp
from jax import lax
from jax.experimental import pallas as pl
from jax.experimental.pallas import tpu as pltpu

F32 = jnp.float32
BF16 = jnp.bfloat16

RMS_EPS = 1e-6
LANES = 128
SUBLANES = 8
CHUNK = 64
SB_HEADS = 8
SG_GROUPS = 8
PEER_HEADS = 8
PEER_KEYS = 128
PEER_TOPK = 16
VMEM_LIMIT = 56 * 1024 * 1024

_NT = (((1,), (1,)), ((), ()))
_LOG2E = 1.0 / math.log(2.0)


def _gelu(x):
    return 0.5 * x * (1.0 + lax.erf(x * (1.0 / math.sqrt(2.0))))


def _rms(x):
    return x * lax.rsqrt(jnp.mean(x * x, axis=-1, keepdims=True) + RMS_EPS)


def _inproj_kernel(x_ref, nm_ref, w_ref, qn_ref, kn_ref, sgn_ref, o_ref, h_ref, *, tiles_per_region):
    j = pl.program_id(1)
    tn = o_ref.shape[1]

    @pl.when(j == 0)
    def _():
        h_ref[...] = (_rms(x_ref[...]) * nm_ref[...]).astype(BF16)

    region = j // tiles_per_region
    mxu_n = 2 * LANES

    def sweep(epilogue):
        for c0 in range(0, tn, mxu_n):
            acc = jnp.dot(h_ref[...], w_ref[:, c0:c0 + mxu_n], preferred_element_type=F32)
            for c in range(c0, c0 + mxu_n, LANES):
                sl = slice(c, c + LANES)
                o_ref[:, sl] = epilogue(acc[:, c - c0:c - c0 + LANES], sl).astype(o_ref.dtype)

    @pl.when(region == 0)
    def _():
        sweep(lambda a, sl: _rms(a) * qn_ref[...])

    @pl.when(region == 1)
    def _():
        sweep(lambda a, sl: _rms(a) * kn_ref[...])

    @pl.when(region == 2)
    def _():
        sweep(lambda a, sl: a)

    @pl.when(region == 3)
    def _():
        sweep(lambda a, sl: _gelu(a))

    @pl.when(region == 4)
    def _():
        sweep(lambda a, sl: _rms(_gelu(a)) * sgn_ref[:, sl])


def _inproj(x2, norm_mix, w_in, q_norm, k_norm, sg_norm, *, tm=512, tn=1024):
    t, d = x2.shape
    n = w_in.shape[1]
    region_w = SB_HEADS * LANES
    tpr = region_w // tn
    return pl.pallas_call(
        functools.partial(_inproj_kernel, tiles_per_region=tpr),
        grid=(t // tm, n // tn),
        in_specs=[
            pl.BlockSpec((tm, d), lambda i, j: (i, 0)),
            pl.BlockSpec((1, d), lambda i, j: (0, 0)),
            pl.BlockSpec((d, tn), lambda i, j: (0, j)),
            pl.BlockSpec((1, LANES), lambda i, j: (0, 0)),
            pl.BlockSpec((1, LANES), lambda i, j: (0, 0)),
            pl.BlockSpec((1, tn), lambda i, j: (0, jnp.maximum(j - 4 * tpr, 0))),
        ],
        out_specs=pl.BlockSpec((tm, tn), lambda i, j: (i, j)),
        out_shape=jax.ShapeDtypeStruct((t, n), BF16),
        scratch_shapes=[pltpu.VMEM((tm, d), BF16)],
        compiler_params=pltpu.CompilerParams(
            dimension_semantics=("parallel", "arbitrary"), vmem_limit_bytes=VMEM_LIMIT),
        name="inproj",
    )(x2, norm_mix, w_in, q_norm, k_norm, sg_norm)


def _attn_kernel(q_ref, k_ref, v_ref, tri_ref, o_ref, acc_ref, c_ref, *, scale):
    i = pl.program_id(1)
    row = lax.broadcasted_iota(jnp.int32, (LANES, LANES), 0)
    col = lax.broadcasted_iota(jnp.int32, (LANES, LANES), 1)
    strict = col < row

    heads = range(SB_HEADS)
    cols = [slice(h * LANES, (h + 1) * LANES) for h in heads]

    def blocks(starts, diag):
        chains = [(start, h) for start in starts for h in heads]
        zs = [lax.dot_general(q_ref[0, :, cols[h]], k_ref[0, pl.ds(start, LANES), cols[h]], _NT,
                              preferred_element_type=F32) * (scale * _LOG2E) for start, h in chains]
        sps = [jnp.maximum(z, 0.0) + jnp.log2(1.0 + jnp.exp2(-jnp.abs(z))) for z in zs]
        if diag:
            sps = [jnp.where(strict, sp, 0.0) for sp in sps]
        css = []
        for sp in sps:
            hi = sp.astype(BF16)
            lo = (sp - hi.astype(F32)).astype(BF16)
            css.append(jnp.dot(jnp.concatenate([hi, lo], axis=1), tri_ref[...], preferred_element_type=F32))
        carried = [None if diag else c_ref[:, cols[h]] for h in heads]
        acc = [None if diag else acc_ref[:, cols[h]] for h in heads]
        for n, (start, h) in enumerate(chains):
            tail = css[n][:, :LANES]
            if carried[h] is not None:
                tail = tail + carried[h]
            w = jnp.exp2(zs[n] - sps[n] - tail)
            if diag:
                w = jnp.where(strict, w, 0.0)
            pv = jnp.dot(w.astype(BF16), v_ref[0, pl.ds(start, LANES), cols[h]], preferred_element_type=F32)
            acc[h] = pv if acc[h] is None else acc[h] + pv
            total = css[n][:, LANES:]
            carried[h] = total if carried[h] is None else carried[h] + total
        for h in heads:
            acc_ref[:, cols[h]] = acc[h]
            c_ref[:, cols[h]] = carried[h]

    blocks([pl.multiple_of(i * LANES, LANES)], True)

    def body(t, carry):
        right = pl.multiple_of((i - 1 - 2 * t) * LANES, LANES)
        blocks([right, pl.multiple_of(right - LANES, LANES)], False)
        return carry

    lax.fori_loop(0, i // 2, body, 0)

    @pl.when(i % 2 == 1)
    def _():
        blocks([0], False)

    o_ref[0] = acc_ref[...].astype(o_ref.dtype)


def _cumsum_matrix():
    j = lax.broadcasted_iota(jnp.int32, (2 * LANES, 2 * LANES), 0) % LANES
    s = lax.broadcasted_iota(jnp.int32, (2 * LANES, 2 * LANES), 1)
    return jnp.where((s >= LANES) | (j > s), 1.0, 0.0).astype(BF16)


def _attn(proj3):
    b, s, _ = proj3.shape
    w = SB_HEADS * LANES
    return pl.pallas_call(
        functools.partial(_attn_kernel, scale=LANES ** -0.5),
        grid=(b, s // LANES),
        in_specs=[
            pl.BlockSpec((1, LANES, w), lambda bi, i: (bi, i, 0)),
            pl.BlockSpec((1, s, w), lambda bi, i: (bi, 0, 1)),
            pl.BlockSpec((1, s, w), lambda bi, i: (bi, 0, 2)),
            pl.BlockSpec((2 * LANES, 2 * LANES), lambda bi, i: (0, 0)),
        ],
        out_specs=pl.BlockSpec((1, LANES, w), lambda bi, i: (bi, i, 0)),
        out_shape=jax.ShapeDtypeStruct((b, s, w), BF16),
        scratch_shapes=[pltpu.VMEM((LANES, w), F32), pltpu.VMEM((LANES, w), F32)],
        compiler_params=pltpu.CompilerParams(
            dimension_semantics=("parallel", "arbitrary"), vmem_limit_bytes=VMEM_LIMIT),
        name="attn",
    )(proj3, proj3, proj3, _cumsum_matrix())


def _mix_out_kernel(ysb_ref, u_ref, vb_ref, sgw_ref, sgb_ref, x_ref, wout_ref, gsb_ref, gsg_ref, nf_ref,
                    x1_ref, h2_ref, ysg_ref):
    tm = x_ref.shape[0]
    p = lax.broadcasted_iota(jnp.int32, (LANES, LANES), 0) // CHUNK
    q = lax.broadcasted_iota(jnp.int32, (LANES, LANES), 1) // CHUNK
    causal = p >= q
    for g in range(SG_GROUPS):
        w = jnp.where(causal, sgw_ref[g], 0.0).astype(BF16)
        cols = slice(g * LANES, (g + 1) * LANES)
        for r in range(tm // LANES):
            rows = slice(r * LANES, (r + 1) * LANES)
            mixed = jnp.dot(w, vb_ref[rows, cols], preferred_element_type=F32) + sgb_ref[g]
            ysg_ref[rows, cols] = u_ref[rows, cols].astype(F32) * mixed
    sbw = ysb_ref.shape[1]
    y_sb = (_rms(ysb_ref[...].astype(F32)) * gsb_ref[...]).astype(BF16)
    y_sg = (_rms(ysg_ref[...]) * gsg_ref[...]).astype(BF16)
    y = jnp.dot(y_sb, wout_ref[:sbw, :], preferred_element_type=F32)
    y = y + jnp.dot(y_sg, wout_ref[sbw:, :], preferred_element_type=F32)
    x1 = x_ref[...] + y
    x1_ref[...] = x1
    h2_ref[...] = (_rms(x1) * nf_ref[...]).astype(BF16)


def _mix_out(y_sb, proj, sg_w, sg_b_bcast, x2, w_out, g_sb, g_sg, norm_ffn, *, tm=256):
    t, d = x2.shape
    sbw = y_sb.shape[1]
    sgw = SG_GROUPS * LANES
    u_blk = (3 * sbw) // sgw
    return pl.pallas_call(
        _mix_out_kernel,
        grid=(t // tm,),
        in_specs=[
            pl.BlockSpec((tm, sbw), lambda i: (i, 0)),
            pl.BlockSpec((tm, sgw), lambda i: (i, u_blk)),
            pl.BlockSpec((tm, sgw), lambda i: (i, u_blk + 1)),
            pl.BlockSpec((SG_GROUPS, LANES, LANES), lambda i: (0, 0, 0)),
            pl.BlockSpec((SG_GROUPS, LANES, LANES), lambda i: (0, 0, 0)),
            pl.BlockSpec((tm, d), lambda i: (i, 0)),
            pl.BlockSpec((sbw + sgw, d), lambda i: (0, 0)),
            pl.BlockSpec((1, sbw), lambda i: (0, 0)),
            pl.BlockSpec((1, sgw), lambda i: (0, 0)),
            pl.BlockSpec((1, d), lambda i: (0, 0)),
        ],
        out_specs=[pl.BlockSpec((tm, d), lambda i: (i, 0)), pl.BlockSpec((tm, d), lambda i: (i, 0))],
        out_shape=[jax.ShapeDtypeStruct((t, d), F32), jax.ShapeDtypeStruct((t, d), BF16)],
        scratch_shapes=[pltpu.VMEM((tm, sgw), F32)],
        compiler_params=pltpu.CompilerParams(
            dimension_semantics=("parallel",), vmem_limit_bytes=VMEM_LIMIT),
        name="mix_out",
    )(y_sb, proj, proj, sg_w, sg_b_bcast, x2, w_out, g_sb, g_sg, norm_ffn)


def _query_kernel(h2_ref, wq_ref, keys_ref, s2_ref, st_ref):
    qf = jnp.dot(h2_ref[...], wq_ref[...], preferred_element_type=F32)
    hw = 2 * PEER_KEYS
    for h in range(PEER_HEADS):
        qn = _rms(qf[:, h * hw:(h + 1) * hw]).astype(BF16)
        for half in range(2):
            qk = qn[:, half * PEER_KEYS:(half + 1) * PEER_KEYS]
            s = lax.dot_general(keys_ref[half], qk, _NT, preferred_element_type=F32)
            if half == 1:
                s2_ref[h] = s
            for c in range(s.shape[1] // LANES):
                st_ref[h, half, :, c, :] = s[:, c * LANES:(c + 1) * LANES]


def _query(h2, w_query, sub_keys, *, tm=SUBLANES * LANES):
    t, d = h2.shape
    return pl.pallas_call(
        _query_kernel,
        grid=(t // tm,),
        in_specs=[
            pl.BlockSpec((tm, d), lambda i: (i, 0)),
            pl.BlockSpec(w_query.shape, lambda i: (0, 0), pipeline_mode=pl.Buffered(1)),
            pl.BlockSpec(sub_keys.shape, lambda i: (0, 0, 0)),
        ],
        out_specs=[
            pl.BlockSpec((PEER_HEADS, PEER_KEYS, tm), lambda i: (0, 0, i)),
            pl.BlockSpec((PEER_HEADS, 2, PEER_KEYS, tm // LANES, LANES), lambda i: (0, 0, 0, i, 0)),
        ],
        out_shape=[
            jax.ShapeDtypeStruct((PEER_HEADS, PEER_KEYS, t), F32),
            jax.ShapeDtypeStruct((PEER_HEADS, 2, PEER_KEYS, t // LANES, LANES), F32),
        ],
        compiler_params=pltpu.CompilerParams(
            dimension_semantics=("parallel",), vmem_limit_bytes=VMEM_LIMIT),
        name="query",
    )(h2, w_query, sub_keys)


def _sort_pairs(n):
    pairs = []
    p = 1
    while p < n:
        k = p
        while k >= 1:
            for j in range(k % p, n - k, 2 * k):
                for i in range(min(k, n - j - k)):
                    if (i + j) // (2 * p) == (i + j + k) // (2 * p):
                        pairs.append((i + j, i + j + k))
            k //= 2
        p *= 2
    return pairs


def _exchange(v, lo, hi):
    a, b = v[lo], v[hi]
    if b is None:
        return
    if a is None:
        v[lo], v[hi] = b, None
        return
    v[lo], v[hi] = jnp.maximum(a, b), jnp.minimum(a, b)


def _sort_desc(v):
    v = list(v)
    for lo, hi in _sort_pairs(len(v)):
        _exchange(v, lo, hi)
    return v


def _merge_top(x, y):
    n = len(x)
    v = [jnp.maximum(x[i], y[n - 1 - i]) for i in range(n)]
    k = n // 2
    while k >= 1:
        for i in range(n):
            if i & k == 0:
                _exchange(v, i, i + k)
        k //= 2
    return v


def _top_desc(vals, k):
    groups = [_sort_desc(vals[g:g + k]) for g in range(0, len(vals), k)]
    while len(groups) > 1:
        groups = [_merge_top(groups[g], groups[g + 1]) for g in range(0, len(groups), 2)]
    return groups[0]


def _topk_kernel(s_ref, cnt_ref, e1_ref, tb_ref):
    k = PEER_TOPK
    a = [s_ref[0, 0, n] for n in range(PEER_KEYS)]
    b = [s_ref[0, 1, n] for n in range(PEER_KEYS)]
    ta = _top_desc(a, k)
    tb = _top_desc(b, k)
    cand = [ta[p] + tb[q] for p in range(k) for q in range(k) if (p + 1) * (q + 1) <= k]
    n_sort = 1 << (len(cand) - 1).bit_length()
    tau = _sort_desc(cand + [None] * (n_sort - len(cand)))[k - 1]
    top = ta[0] + tb[0]
    z = jnp.zeros_like(top)
    for c in cand:
        z = z + jnp.where(c >= tau, jnp.exp(c - top), 0.0)
    half_inv_z = 0.5 / z
    for q in range(k):
        tb_ref[0, q] = tb[q]
    for n in range(PEER_KEYS):
        cnt = jnp.zeros_like(top)
        for q in range(k):
            cnt = jnp.where(a[n] + tb[q] >= tau, float(q + 1), cnt)
        cnt_ref[0, n] = cnt
        e1_ref[0, n] = jnp.exp(a[n] - ta[0]) * half_inv_z


def _topk(st5):
    hh, _, nk, nc, _ = st5.shape
    blk = (1, nk, SUBLANES, LANES)
    return pl.pallas_call(
        _topk_kernel,
        grid=(hh, nc // SUBLANES),
        in_specs=[pl.BlockSpec((1, 2, nk, SUBLANES, LANES), lambda h, u: (h, 0, 0, u, 0))],
        out_specs=[
            pl.BlockSpec(blk, lambda h, u: (h, 0, u, 0)),
            pl.BlockSpec(blk, lambda h, u: (h, 0, u, 0)),
            pl.BlockSpec((1, PEER_TOPK, SUBLANES, LANES), lambda h, u: (h, 0, u, 0)),
        ],
        out_shape=[
            jax.ShapeDtypeStruct((hh, nk, nc, LANES), F32),
            jax.ShapeDtypeStruct((hh, nk, nc, LANES), F32),
            jax.ShapeDtypeStruct((hh, PEER_TOPK, nc, LANES), F32),
        ],
        compiler_params=pltpu.CompilerParams(
            dimension_semantics=("parallel", "parallel"), vmem_limit_bytes=VMEM_LIMIT),
        name="topk",
    )(st5)


_PACK_ROWS = 16
PEER_EXPERT_TILE = 512
_GATE_BLOCKS = 4


def _peer_kernel(h2_ref, u_ref, vt_ref, s2_ref, tb_ref, cnt_ref, e1_ref, x1_ref, o_ref,
                 st0_ref, st1_ref, w0_ref, w1_ref, r2_ref, e2_ref, acc_ref, *, nj, n_tiles):
    s = pl.program_id(0)
    st_ref = (st0_ref, st1_ref)
    w_ref = (w0_ref, w1_ref)
    te, tm = w0_ref.shape
    n_chunks = tm // LANES
    n_l = te // PEER_KEYS
    heads = range(PEER_HEADS)
    sb = jnp.clip(s - 1, 0, n_tiles - 1)
    sc = jnp.clip(s - 2, 0, n_tiles - 1)
    ib, jb = sb // nj, sb % nj
    chunk0 = (ib % (SUBLANES // n_chunks)) * n_chunks

    @pl.when(s == 0)
    def _():
        st1_ref[...] = jnp.zeros_like(st1_ref)
        w0_ref[...] = jnp.zeros_like(w0_ref)

    @pl.when(jb == 0)
    def _():
        for h in heads:
            for c in range(n_chunks):
                lanes = slice(c * LANES, (c + 1) * LANES)
                s2 = s2_ref[h, :, lanes]
                top = lambda q: tb_ref[h, q, pl.ds(chunk0 + c, 1), :]
                rank = jnp.zeros_like(s2)
                for q in range(PEER_TOPK):
                    rank = rank + jnp.where(top(q) > s2, 1.0, 0.0)
                r2_ref[h, :, lanes] = rank.astype(BF16)
                e2_ref[h, :, lanes] = jnp.exp(s2 - top(0)).astype(BF16)

    @pl.when(sc % nj == 0)
    def _():
        acc_ref[...] = jnp.zeros_like(acc_ref)

    d = acc_ref.shape[0]
    halves = (0, 1)

    def step(par):
        def scores(mh, nt):
            m = slice(mh * (te // 2), (mh + 1) * (te // 2))
            n = slice(nt * (tm // 2), (nt + 1) * (tm // 2))
            st_ref[par][m, n] = lax.dot_general(u_ref[m, :], h2_ref[n, :], _NT, preferred_element_type=F32)

        def accumulate(mh, nt):
            m = slice(mh * (d // 2), (mh + 1) * (d // 2))
            n = slice(nt * (tm // 2), (nt + 1) * (tm // 2))
            acc_ref[m, n] += jnp.dot(vt_ref[m, :], w_ref[par][:, n], preferred_element_type=F32)

        def gate_rows(c, r0):
            lanes = slice(c * LANES, (c + 1) * LANES)
            blocks = range(r0, r0 + _GATE_BLOCKS)
            keys = [slice(r * _PACK_ROWS, (r + 1) * _PACK_ROWS) for r in blocks]
            gates = [[None] * _GATE_BLOCKS for _ in range(n_l)]
            for h in heads:
                rank = [r2_ref[h, k, lanes] for k in keys]
                e2 = [e2_ref[h, k, lanes] for k in keys]
                for l in range(n_l):
                    i1 = jb * n_l + l
                    row = lambda ref: jnp.broadcast_to(
                        ref[h, i1, pl.ds(chunk0 + c, 1), :], (_PACK_ROWS, LANES)).astype(BF16)
                    cnt, e1 = row(cnt_ref), row(e1_ref)
                    for b in range(_GATE_BLOCKS):
                        term = jnp.where(rank[b] < cnt, e2[b], jnp.zeros_like(e2[b])) * e1
                        gates[l][b] = term if gates[l][b] is None else gates[l][b] + term
            for l in range(n_l):
                for b, r in enumerate(blocks):
                    rows = slice(l * PEER_KEYS + r * _PACK_ROWS, l * PEER_KEYS + (r + 1) * _PACK_ROWS)
                    a = st_ref[1 - par][rows, lanes]
                    act = (a * (1.0 + lax.erf(a * (1.0 / math.sqrt(2.0))))).astype(BF16)
                    w_ref[1 - par][rows, lanes] = act * gates[l][b]

        mxu_work = []
        for nt in halves:
            for mh in halves:
                mxu_work += [functools.partial(scores, mh, nt), functools.partial(accumulate, mh, nt)]
        vpu_work = [functools.partial(gate_rows, c, r0)
                    for r0 in range(0, PEER_KEYS // _PACK_ROWS, _GATE_BLOCKS) for c in range(n_chunks)]
        per = len(vpu_work) // len(mxu_work)
        for k, mm in enumerate(mxu_work):
            mm()
            for blk in vpu_work[k * per:(k + 1) * per]:
                blk()

    @pl.when(s % 2 == 0)
    def _():
        step(0)

    @pl.when(s % 2 == 1)
    def _():
        step(1)

    @pl.when((s >= 2) & (sc % nj == nj - 1))
    def _():
        for n in range(tm // LANES):
            rows = slice(n * LANES, (n + 1) * LANES)
            for m in range(d // LANES):
                cols = slice(m * LANES, (m + 1) * LANES)
                o_ref[rows, cols] = x1_ref[rows, cols] + acc_ref[cols, rows].T


def _peer(h2, u_bf, vt_bf, s2, tb, cnt, e1, x1, *, tm=512):
    t, d = h2.shape
    nk = PEER_KEYS
    nj, _, te = vt_bf.shape
    n_tiles = (t // tm) * nj
    tile_a = lambda s: jnp.minimum(s, n_tiles - 1)
    tile_b = lambda s: jnp.clip(s - 1, 0, n_tiles - 1)
    tile_c = lambda s: jnp.clip(s - 2, 0, n_tiles - 1)
    big = lambda s: (tile_b(s) // nj) * tm // (SUBLANES * LANES)
    return pl.pallas_call(
        functools.partial(_peer_kernel, nj=nj, n_tiles=n_tiles),
        grid=(n_tiles + 2,),
        in_specs=[
            pl.BlockSpec((tm, d), lambda s: (tile_a(s) // nj, 0)),
            pl.BlockSpec((te, d), lambda s: (tile_a(s) % nj, 0)),
            pl.BlockSpec((None, d, te), lambda s: (tile_c(s) % nj, 0, 0)),
            pl.BlockSpec((PEER_HEADS, nk, tm), lambda s: (0, 0, tile_b(s) // nj)),
            pl.BlockSpec((PEER_HEADS, PEER_TOPK, SUBLANES, LANES), lambda s: (0, 0, big(s), 0)),
            pl.BlockSpec((PEER_HEADS, nk, SUBLANES, LANES), lambda s: (0, 0, big(s), 0)),
            pl.BlockSpec((PEER_HEADS, nk, SUBLANES, LANES), lambda s: (0, 0, big(s), 0)),
            pl.BlockSpec((tm, d), lambda s: (tile_c(s) // nj, 0), pipeline_mode=pl.Buffered(1)),
        ],
        out_specs=pl.BlockSpec((tm, d), lambda s: (tile_c(s) // nj, 0)),
        out_shape=jax.ShapeDtypeStruct((t, d), F32),
        scratch_shapes=[pltpu.VMEM((te, tm), F32), pltpu.VMEM((te, tm), F32),
                        pltpu.VMEM((te, tm), BF16), pltpu.VMEM((te, tm), BF16),
                        pltpu.VMEM((PEER_HEADS, nk, tm + LANES), BF16),
                        pltpu.VMEM((PEER_HEADS, nk, tm + LANES), BF16),
                        pltpu.VMEM((d, tm), F32)],
        compiler_params=pltpu.CompilerParams(
            dimension_semantics=("arbitrary",), vmem_limit_bytes=VMEM_LIMIT),
        name="peer",
    )(h2, u_bf, vt_bf, s2, tb, cnt, e1, x1)


def _transpose_tiles_kernel(x_ref, o_ref):
    blk = o_ref.shape[1]
    for c0 in range(0, o_ref.shape[0], blk):
        o_ref[c0:c0 + blk, :] = x_ref[:, c0:c0 + blk].T.astype(o_ref.dtype)


def _transpose_tiles(x, dtype, *, te):
    r, c = x.shape
    return pl.pallas_call(
        _transpose_tiles_kernel,
        grid=(r // te,),
        in_specs=[pl.BlockSpec((te, c), lambda j: (j, 0))],
        out_specs=pl.BlockSpec((None, c, te), lambda j: (j, 0, 0)),
        out_shape=jax.ShapeDtypeStruct((r // te, c, te), dtype),
        compiler_params=pltpu.CompilerParams(
            dimension_semantics=("parallel",), vmem_limit_bytes=VMEM_LIMIT),
        name="transpose_tiles",
    )(x)


def _layer(x, norm_mix, w_in, q_norm, k_norm, sg_norm, sg_w, sg_b, out_norm_sb, out_norm_sg,
           w_out, norm_ffn, peer_w_query, peer_sub_keys, peer_u, peer_v):
    b, s, d = x.shape
    t = b * s
    x2 = x.reshape(t, d)
    row = lambda v: v.reshape(1, -1)

    proj = _inproj(x2, row(norm_mix), w_in.astype(BF16), row(q_norm), row(k_norm), row(sg_norm))
    y_sb = _attn(proj.reshape(b, s, -1)).reshape(t, -1)
    sg_b_bcast = jnp.broadcast_to(sg_b[:, :, None], sg_w.shape)
    x1, h2 = _mix_out(y_sb, proj, sg_w, sg_b_bcast, x2, w_out.astype(BF16),
                      row(out_norm_sb), row(out_norm_sg), row(norm_ffn))

    s2, st5 = _query(h2, peer_w_query.astype(BF16), peer_sub_keys.astype(BF16))
    cnt, e1, tb = _topk(st5)
    vt = _transpose_tiles(peer_v, BF16, te=PEER_EXPERT_TILE)
    out = _peer(h2, peer_u.astype(BF16), vt, s2, tb, cnt, e1, x1)
    return out.reshape(b, s, d)


def kernel(x, norm_mix, w_in, q_norm, k_norm, sg_norm, sg_w, sg_b, out_norm_sb, out_norm_sg,
           w_out, norm_ffn, peer_w_query, peer_sub_keys, peer_u, peer_v):
    for layer in range(norm_mix.shape[0]):
        x = _layer(x, norm_mix[layer], w_in[layer], q_norm[layer], k_norm[layer], sg_norm[layer],
                   sg_w[layer], sg_b[layer], out_norm_sb[layer], out_norm_sg[layer], w_out[layer],
                   norm_ffn[layer], peer_w_query[layer], peer_sub_keys[layer], peer_u[layer],
                   peer_v[layer])
    return x
```

```python
import functools
import math

import jax
import jax.numpy as jnp
from jax import lax
from jax.experimental import pallas as pl
from jax.experimental.pallas import tpu as pltpu

F32 = jnp.float32
BF16 = jnp.bfloat16

RMS_EPS = 1e-6
LANES = 128
SUBLANES = 8
CHUNK = 64
SB_HEADS = 8
SG_GROUPS = 8
PEER_HEADS = 8
PEER_KEYS = 128
PEER_TOPK = 16
VMEM_LIMIT = 56 * 1024 * 1024

_NT = (((1,), (1,)), ((), ()))
_LOG2E = 1.0 / math.log(2.0)


def _gelu(x):
    return 0.5 * x * (1.0 + lax.erf(x * (1.0 / math.sqrt(2.0))))


def _rms(x):
    return x * lax.rsqrt(jnp.mean(x * x, axis=-1, keepdims=True) + RMS_EPS)


def _inproj_kernel(x_ref, nm_ref, w_ref, qn_ref, kn_ref, sgn_ref, o_ref, h_ref, *, tiles_per_region):
    j = pl.program_id(1)
    tn = o_ref.shape[1]

    @pl.when(j == 0)
    def _():
        h_ref[...] = (_rms(x_ref[...]) * nm_ref[...]).astype(BF16)

    region = j // tiles_per_region
    mxu_n = 2 * LANES

    def sweep(epilogue):
        for c0 in range(0, tn, mxu_n):
            acc = jnp.dot(h_ref[...], w_ref[:, c0:c0 + mxu_n], preferred_element_type=F32)
            for c in range(c0, c0 + mxu_n, LANES):
                sl = slice(c, c + LANES)
                o_ref[:, sl] = epilogue(acc[:, c - c0:c - c0 + LANES], sl).astype(o_ref.dtype)

    @pl.when(region == 0)
    def _():
        sweep(lambda a, sl: _rms(a) * qn_ref[...])

    @pl.when(region == 1)
    def _():
        sweep(lambda a, sl: _rms(a) * kn_ref[...])

    @pl.when(region == 2)
    def _():
        sweep(lambda a, sl: a)

    @pl.when(region == 3)
    def _():
        sweep(lambda a, sl: _gelu(a))

    @pl.when(region == 4)
    def _():
        sweep(lambda a, sl: _rms(_gelu(a)) * sgn_ref[:, sl])


def _inproj(x2, norm_mix, w_in, q_norm, k_norm, sg_norm, *, tm=1024, tn=1024):
    t, d = x2.shape
    n = w_in.shape[1]
    region_w = SB_HEADS * LANES
    tpr = region_w // tn
    return pl.pallas_call(
        functools.partial(_inproj_kernel, tiles_per_region=tpr),
        grid=(t // tm, n // tn),
        in_specs=[
            pl.BlockSpec((tm, d), lambda i, j: (i, 0)),
            pl.BlockSpec((1, d), lambda i, j: (0, 0)),
            pl.BlockSpec((d, tn), lambda i, j: (0, j)),
            pl.BlockSpec((1, LANES), lambda i, j: (0, 0)),
            pl.BlockSpec((1, LANES), lambda i, j: (0, 0)),
            pl.BlockSpec((1, tn), lambda i, j: (0, jnp.maximum(j - 4 * tpr, 0))),
        ],
        out_specs=pl.BlockSpec((tm, tn), lambda i, j: (i, j)),
        out_shape=jax.ShapeDtypeStruct((t, n), BF16),
        scratch_shapes=[pltpu.VMEM((tm, d), BF16)],
        compiler_params=pltpu.CompilerParams(
            dimension_semantics=("parallel", "arbitrary"), vmem_limit_bytes=VMEM_LIMIT),
        name="inproj",
    )(x2, norm_mix, w_in, q_norm, k_norm, sg_norm)


def _attn_kernel(q_ref, k_ref, v_ref, tri_ref, o_ref, acc_ref, c_ref, *, scale):
    i = pl.program_id(1)
    row = lax.broadcasted_iota(jnp.int32, (LANES, LANES), 0)
    col = lax.broadcasted_iota(jnp.int32, (LANES, LANES), 1)
    strict = col < row

    heads = range(SB_HEADS)
    cols = [slice(h * LANES, (h + 1) * LANES) for h in heads]

    def blocks(starts, diag):
        chains = [(start, h) for start in starts for h in heads]
        zs = [lax.dot_general(q_ref[0, :, cols[h]], k_ref[0, pl.ds(start, LANES), cols[h]], _NT,
                              preferred_element_type=F32) * (scale * _LOG2E) for start, h in chains]
        sps = [jnp.maximum(z, 0.0) + jnp.log2(1.0 + jnp.exp2(-jnp.abs(z))) for z in zs]
        if diag:
            sps = [jnp.where(strict, sp, 0.0) for sp in sps]
        css = []
        for sp in sps:
            hi = sp.astype(BF16)
            lo = (sp - hi.astype(F32)).astype(BF16)
            css.append(jnp.dot(jnp.concatenate([hi, lo], axis=1), tri_ref[...], preferred_element_type=F32))
        carried = [None if diag else c_ref[:, cols[h]] for h in heads]
        acc = [None if diag else acc_ref[:, cols[h]] for h in heads]
        for n, (start, h) in enumerate(chains):
            tail = css[n][:, :LANES]
            if carried[h] is not None:
                tail = tail + carried[h]
            w = jnp.exp2(zs[n] - sps[n] - tail)
            if diag:
                w = jnp.where(strict, w, 0.0)
            pv = jnp.dot(w.astype(BF16), v_ref[0, pl.ds(start, LANES), cols[h]], preferred_element_type=F32)
            acc[h] = pv if acc[h] is None else acc[h] + pv
            total = css[n][:, LANES:]
            carried[h] = total if carried[h] is None else carried[h] + total
        for h in heads:
            acc_ref[:, cols[h]] = acc[h]
            c_ref[:, cols[h]] = carried[h]

    blocks([pl.multiple_of(i * LANES, LANES)], True)

    def body(t, carry):
        right = pl.multiple_of((i - 1 - 2 * t) * LANES, LANES)
        blocks([right, pl.multiple_of(right - LANES, LANES)], False)
        return carry

    lax.fori_loop(0, i // 2, body, 0)

    @pl.when(i % 2 == 1)
    def _():
        blocks([0], False)

    o_ref[0] = acc_ref[...].astype(o_ref.dtype)


def _cumsum_matrix():
    j = lax.broadcasted_iota(jnp.int32, (2 * LANES, 2 * LANES), 0) % LANES
    s = lax.broadcasted_iota(jnp.int32, (2 * LANES, 2 * LANES), 1)
    return jnp.where((s >= LANES) | (j > s), 1.0, 0.0).astype(BF16)


def _attn(proj3):
    b, s, _ = proj3.shape
    w = SB_HEADS * LANES
    return pl.pallas_call(
        functools.partial(_attn_kernel, scale=LANES ** -0.5),
        grid=(b, s // LANES),
        in_specs=[
            pl.BlockSpec((1, LANES, w), lambda bi, i: (bi, i, 0)),
            pl.BlockSpec((1, s, w), lambda bi, i: (bi, 0, 1)),
            pl.BlockSpec((1, s, w), lambda bi, i: (bi, 0, 2)),
            pl.BlockSpec((2 * LANES, 2 * LANES), lambda bi, i: (0, 0)),
        ],
        out_specs=pl.BlockSpec((1, LANES, w), lambda bi, i: (bi, i, 0)),
        out_shape=jax.ShapeDtypeStruct((b, s, w), BF16),
        scratch_shapes=[pltpu.VMEM((LANES, w), F32), pltpu.VMEM((LANES, w), F32)],
        compiler_params=pltpu.CompilerParams(
            dimension_semantics=("parallel", "arbitrary"), vmem_limit_bytes=VMEM_LIMIT),
        name="attn",
    )(proj3, proj3, proj3, _cumsum_matrix())


def _mix_out_kernel(ysb_ref, u_ref, vb_ref, sgw_ref, sgb_ref, x_ref, wout_ref, gsb_ref, gsg_ref, nf_ref,
                    x1_ref, h2_ref, h2t_ref, ysg_ref):
    tm = x_ref.shape[0]
    p = lax.broadcasted_iota(jnp.int32, (LANES, LANES), 0) // CHUNK
    q = lax.broadcasted_iota(jnp.int32, (LANES, LANES), 1) // CHUNK
    causal = p >= q
    for g in range(SG_GROUPS):
        w = jnp.where(causal, sgw_ref[g], 0.0).astype(BF16)
        cols = slice(g * LANES, (g + 1) * LANES)
        for r in range(tm // LANES):
            rows = slice(r * LANES, (r + 1) * LANES)
            mixed = jnp.dot(w, vb_ref[rows, cols], preferred_element_type=F32) + sgb_ref[g]
            ysg_ref[rows, cols] = u_ref[rows, cols].astype(F32) * mixed
    sbw = ysb_ref.shape[1]
    y_sb = (_rms(ysb_ref[...].astype(F32)) * gsb_ref[...]).astype(BF16)
    y_sg = (_rms(ysg_ref[...]) * gsg_ref[...]).astype(BF16)
    y = jnp.dot(y_sb, wout_ref[:sbw, :], preferred_element_type=F32)
    y = y + jnp.dot(y_sg, wout_ref[sbw:, :], preferred_element_type=F32)
    x1 = x_ref[...] + y
    x1_ref[...] = x1
    h2 = _rms(x1) * nf_ref[...]
    h2_ref[...] = h2.astype(BF16)
    for c in range(0, h2.shape[1], LANES):
        h2t_ref[c:c + LANES, :] = h2[:, c:c + LANES].T.astype(BF16)


def _mix_out(y_sb, proj, sg_w, sg_b_bcast, x2, w_out, g_sb, g_sg, norm_ffn, *, tm=512):
    t, d = x2.shape
    sbw = y_sb.shape[1]
    sgw = SG_GROUPS * LANES
    u_blk = (3 * sbw) // sgw
    return pl.pallas_call(
        _mix_out_kernel,
        grid=(t // tm,),
        in_specs=[
            pl.BlockSpec((tm, sbw), lambda i: (i, 0)),
            pl.BlockSpec((tm, sgw), lambda i: (i, u_blk)),
            pl.BlockSpec((tm, sgw), lambda i: (i, u_blk + 1)),
            pl.BlockSpec((SG_GROUPS, LANES, LANES), lambda i: (0, 0, 0)),
            pl.BlockSpec((SG_GROUPS, LANES, LANES), lambda i: (0, 0, 0)),
            pl.BlockSpec((tm, d), lambda i: (i, 0)),
            pl.BlockSpec((sbw + sgw, d), lambda i: (0, 0)),
            pl.BlockSpec((1, sbw), lambda i: (0, 0)),
            pl.BlockSpec((1, sgw), lambda i: (0, 0)),
            pl.BlockSpec((1, d), lambda i: (0, 0)),
        ],
        out_specs=[pl.BlockSpec((tm, d), lambda i: (i, 0)), pl.BlockSpec((tm, d), lambda i: (i, 0)),
                   pl.BlockSpec((d, tm), lambda i: (0, i))],
        out_shape=[jax.ShapeDtypeStruct((t, d), F32), jax.ShapeDtypeStruct((t, d), BF16),
                   jax.ShapeDtypeStruct((d, t), BF16)],
        scratch_shapes=[pltpu.VMEM((tm, sgw), F32)],
        compiler_params=pltpu.CompilerParams(
            dimension_semantics=("parallel",), vmem_limit_bytes=VMEM_LIMIT),
        name="mix_out",
    )(y_sb, proj, proj, sg_w, sg_b_bcast, x2, w_out, g_sb, g_sg, norm_ffn)


def _query_kernel(h2_ref, wq_ref, keys_ref, s2_ref, st_ref):
    qf = jnp.dot(h2_ref[...], wq_ref[...], preferred_element_type=F32)
    hw = 2 * PEER_KEYS
    for h in range(PEER_HEADS):
        qn = _rms(qf[:, h * hw:(h + 1) * hw]).astype(BF16)
        for half in range(2):
            qk = qn[:, half * PEER_KEYS:(half + 1) * PEER_KEYS]
            s = lax.dot_general(keys_ref[half], qk, _NT, preferred_element_type=F32)
            if half == 1:
                s2_ref[h] = s
            for c in range(s.shape[1] // LANES):
                st_ref[h, half, :, c, :] = s[:, c * LANES:(c + 1) * LANES]


def _query(h2, w_query, sub_keys, *, tm=SUBLANES * LANES):
    t, d = h2.shape
    return pl.pallas_call(
        _query_kernel,
        grid=(t // tm,),
        in_specs=[
            pl.BlockSpec((tm, d), lambda i: (i, 0)),
            pl.BlockSpec(w_query.shape, lambda i: (0, 0), pipeline_mode=pl.Buffered(1)),
            pl.BlockSpec(sub_keys.shape, lambda i: (0, 0, 0)),
        ],
        out_specs=[
            pl.BlockSpec((PEER_HEADS, PEER_KEYS, tm), lambda i: (0, 0, i)),
            pl.BlockSpec((PEER_HEADS, 2, PEER_KEYS, tm // LANES, LANES), lambda i: (0, 0, 0, i, 0)),
        ],
        out_shape=[
            jax.ShapeDtypeStruct((PEER_HEADS, PEER_KEYS, t), F32),
            jax.ShapeDtypeStruct((PEER_HEADS, 2, PEER_KEYS, t // LANES, LANES), F32),
        ],
        compiler_params=pltpu.CompilerParams(
            dimension_semantics=("parallel",), vmem_limit_bytes=VMEM_LIMIT),
        name="query",
    )(h2, w_query, sub_keys)


def _sort_pairs(n):
    pairs = []
    p = 1
    while p < n:
        k = p
        while k >= 1:
            for j in range(k % p, n - k, 2 * k):
                for i in range(min(k, n - j - k)):
                    if (i + j) // (2 * p) == (i + j + k) // (2 * p):
                        pairs.append((i + j, i + j + k))
            k //= 2
        p *= 2
    return pairs


def _exchange(v, lo, hi):
    a, b = v[lo], v[hi]
    if b is None:
        return
    if a is None:
        v[lo], v[hi] = b, None
        return
    v[lo], v[hi] = jnp.maximum(a, b), jnp.minimum(a, b)


def _sort_desc(v):
    v = list(v)
    for lo, hi in _sort_pairs(len(v)):
        _exchange(v, lo, hi)
    return v


def _merge_top(x, y):
    n = len(x)
    v = [jnp.maximum(x[i], y[n - 1 - i]) for i in range(n)]
    k = n // 2
    while k >= 1:
        for i in range(n):
            if i & k == 0:
                _exchange(v, i, i + k)
        k //= 2
    return v


def _top_desc(vals, k):
    groups = [_sort_desc(vals[g:g + k]) for g in range(0, len(vals), k)]
    while len(groups) > 1:
        groups = [_merge_top(groups[g], groups[g + 1]) for g in range(0, len(groups), 2)]
    return groups[0]


def _topk_kernel(s_ref, th_ref, e1_ref, m2_ref):
    k = PEER_TOPK
    a = [s_ref[0, 0, n] for n in range(PEER_KEYS)]
    b = [s_ref[0, 1, n] for n in range(PEER_KEYS)]
    ta = _top_desc(a, k)
    tb = _top_desc(b, k)
    cand = [ta[p] + tb[q] for p in range(k) for q in range(k) if (p + 1) * (q + 1) <= k]
    n_sort = 1 << (len(cand) - 1).bit_length()
    tau = _sort_desc(cand + [None] * (n_sort - len(cand)))[k - 1]
    top = ta[0] + tb[0]
    z = jnp.zeros_like(top)
    for c in cand:
        z = z + jnp.where(c >= tau, jnp.exp(c - top), 0.0)
    half_inv_z = 0.5 / z
    m2_ref[0] = tb[0]
    for n in range(PEER_KEYS):
        th = jnp.full_like(top, jnp.inf)
        for q in range(k):
            th = jnp.where(a[n] + tb[q] >= tau, tb[q], th)
        th_ref[0, n] = th
        e1_ref[0, n] = jnp.exp(a[n] - ta[0]) * half_inv_z


def _topk(st5):
    hh, _, nk, nc, _ = st5.shape
    blk = (1, nk, SUBLANES, LANES)
    return pl.pallas_call(
        _topk_kernel,
        grid=(hh, nc // SUBLANES),
        in_specs=[pl.BlockSpec((1, 2, nk, SUBLANES, LANES), lambda h, u: (h, 0, 0, u, 0))],
        out_specs=[
            pl.BlockSpec(blk, lambda h, u: (h, 0, u, 0)),
            pl.BlockSpec(blk, lambda h, u: (h, 0, u, 0)),
            pl.BlockSpec((1, SUBLANES, LANES), lambda h, u: (h, u, 0)),
        ],
        out_shape=[
            jax.ShapeDtypeStruct((hh, nk, nc, LANES), F32),
            jax.ShapeDtypeStruct((hh, nk, nc, LANES), F32),
            jax.ShapeDtypeStruct((hh, nc, LANES), F32),
        ],
        compiler_params=pltpu.CompilerParams(
            dimension_semantics=("parallel", "parallel"), vmem_limit_bytes=VMEM_LIMIT),
        name="topk",
    )(st5)


_PACK_ROWS = 16
PEER_EXPERT_TILE = 512


def _peer_kernel(h2t_ref, u_ref, vt_ref, s2_ref, m2_ref, th_ref, e1_ref, x1_ref, o_ref,
                 st0_ref, st1_ref, w0_ref, w1_ref, e2_ref, acc_ref, *, nj, n_tiles):
    s = pl.program_id(0)
    st_ref = (st0_ref, st1_ref)
    w_ref = (w0_ref, w1_ref)
    te, tm = w0_ref.shape
    n_chunks = tm // LANES
    n_l = te // PEER_KEYS
    heads = range(PEER_HEADS)
    sb = jnp.clip(s - 1, 0, n_tiles - 1)
    sc = jnp.clip(s - 2, 0, n_tiles - 1)
    ib, jb = sb // nj, sb % nj
    chunk0 = (ib % (SUBLANES // n_chunks)) * n_chunks

    @pl.when(s == 0)
    def _():
        st1_ref[...] = jnp.zeros_like(st1_ref)
        w0_ref[...] = jnp.zeros_like(w0_ref)

    @pl.when(jb == 0)
    def _():
        for h in heads:
            for c in range(n_chunks):
                lanes = slice(c * LANES, (c + 1) * LANES)
                e2_ref[h, :, lanes] = jnp.exp(s2_ref[h, :, lanes] - m2_ref[h, pl.ds(chunk0 + c, 1), :])

    @pl.when(sc % nj == 0)
    def _():
        acc_ref[...] = jnp.zeros_like(acc_ref)

    d = acc_ref.shape[0]
    halves = (0, 1)

    def step(par):
        def scores(mh, nt):
            m = slice(mh * (te // 2), (mh + 1) * (te // 2))
            n = slice(nt * (tm // 2), (nt + 1) * (tm // 2))
            st_ref[par][m, n] = jnp.dot(u_ref[m, :], h2t_ref[:, n], preferred_element_type=F32)

        def accumulate(mh, nt):
            m = slice(mh * (d // 2), (mh + 1) * (d // 2))
            n = slice(nt * (tm // 2), (nt + 1) * (tm // 2))
            acc_ref[m, n] += jnp.dot(vt_ref[m, :], w_ref[par][:, n], preferred_element_type=F32)

        def gate_rows(c, r):
            lanes = slice(c * LANES, (c + 1) * LANES)
            keys = slice(r * _PACK_ROWS, (r + 1) * _PACK_ROWS)
            gates = [None] * n_l
            for h in heads:
                s2 = s2_ref[h, keys, lanes]
                e2 = e2_ref[h, keys, lanes]
                for l in range(n_l):
                    i1 = jb * n_l + l
                    th = th_ref[h, i1, pl.ds(chunk0 + c, 1), :]
                    e1 = e1_ref[h, i1, pl.ds(chunk0 + c, 1), :]
                    term = jnp.where(s2 >= th, e2, 0.0) * e1
                    gates[l] = term if gates[l] is None else gates[l] + term
            for l in range(n_l):
                rows = slice(l * PEER_KEYS + r * _PACK_ROWS, l * PEER_KEYS + (r + 1) * _PACK_ROWS)
                a = st_ref[1 - par][rows, lanes]
                w_ref[1 - par][rows, lanes] = (
                    a * (1.0 + lax.erf(a * (1.0 / math.sqrt(2.0)))) * gates[l]).astype(BF16)

        mxu_work = []
        for nt in halves:
            for mh in halves:
                mxu_work += [functools.partial(scores, mh, nt), functools.partial(accumulate, mh, nt)]
        vpu_work = [functools.partial(gate_rows, c, r) for r in range(PEER_KEYS // _PACK_ROWS)
                    for c in range(n_chunks)]
        per = len(vpu_work) // len(mxu_work)
        for k, mm in enumerate(mxu_work):
            mm()
            for blk in vpu_work[k * per:(k + 1) * per]:
                blk()

    @pl.when(s % 2 == 0)
    def _():
        step(0)

    @pl.when(s % 2 == 1)
    def _():
        step(1)

    @pl.when((s >= 2) & (sc % nj == nj - 1))
    def _():
        for n in range(tm // LANES):
            rows = slice(n * LANES, (n + 1) * LANES)
            for m in range(d // LANES):
                cols = slice(m * LANES, (m + 1) * LANES)
                o_ref[rows, cols] = x1_ref[rows, cols] + acc_ref[cols, rows].T


def _peer(h2t, u_bf, vt_bf, s2, m2, th, e1, x1, *, tm=512):
    d, t = h2t.shape
    nk = PEER_KEYS
    nj, _, te = vt_bf.shape
    n_tiles = (t // tm) * nj
    tile_a = lambda s: jnp.minimum(s, n_tiles - 1)
    tile_b = lambda s: jnp.clip(s - 1, 0, n_tiles - 1)
    tile_c = lambda s: jnp.clip(s - 2, 0, n_tiles - 1)
    big = lambda s: (tile_b(s) // nj) * tm // (SUBLANES * LANES)
    return pl.pallas_call(
        functools.partial(_peer_kernel, nj=nj, n_tiles=n_tiles),
        grid=(n_tiles + 2,),
        in_specs=[
            pl.BlockSpec((d, tm), lambda s: (0, tile_a(s) // nj)),
            pl.BlockSpec((te, d), lambda s: (tile_a(s) % nj, 0)),
            pl.BlockSpec((None, d, te), lambda s: (tile_c(s) % nj, 0, 0)),
            pl.BlockSpec((PEER_HEADS, nk, tm), lambda s: (0, 0, tile_b(s) // nj)),
            pl.BlockSpec((PEER_HEADS, SUBLANES, LANES), lambda s: (0, big(s), 0)),
            pl.BlockSpec((PEER_HEADS, nk, SUBLANES, LANES), lambda s: (0, 0, big(s), 0)),
            pl.BlockSpec((PEER_HEADS, nk, SUBLANES, LANES), lambda s: (0, 0, big(s), 0)),
            pl.BlockSpec((tm, d), lambda s: (tile_c(s) // nj, 0), pipeline_mode=pl.Buffered(1)),
        ],
        out_specs=pl.BlockSpec((tm, d), lambda s: (tile_c(s) // nj, 0)),
        out_shape=jax.ShapeDtypeStruct((t, d), F32),
        scratch_shapes=[pltpu.VMEM((te, tm), F32), pltpu.VMEM((te, tm), F32),
                        pltpu.VMEM((te, tm), BF16), pltpu.VMEM((te, tm), BF16),
                        pltpu.VMEM((PEER_HEADS, nk, tm), F32), pltpu.VMEM((d, tm), F32)],
        compiler_params=pltpu.CompilerParams(
            dimension_semantics=("arbitrary",), vmem_limit_bytes=VMEM_LIMIT),
        name="peer",
    )(h2t, u_bf, vt_bf, s2, m2, th, e1, x1)


def _transpose_tiles_kernel(x_ref, o_ref):
    blk = o_ref.shape[1]
    for c0 in range(0, o_ref.shape[0], blk):
        o_ref[c0:c0 + blk, :] = x_ref[:, c0:c0 + blk].T.astype(o_ref.dtype)


def _transpose_tiles(x, dtype, *, te):
    r, c = x.shape
    return pl.pallas_call(
        _transpose_tiles_kernel,
        grid=(r // te,),
        in_specs=[pl.BlockSpec((te, c), lambda j: (j, 0))],
        out_specs=pl.BlockSpec((None, c, te), lambda j: (j, 0, 0)),
        out_shape=jax.ShapeDtypeStruct((r // te, c, te), dtype),
        compiler_params=pltpu.CompilerParams(
            dimension_semantics=("parallel",), vmem_limit_bytes=VMEM_LIMIT),
        name="transpose_tiles",
    )(x)


def _layer(x, norm_mix, w_in, q_norm, k_norm, sg_norm, sg_w, sg_b, out_norm_sb, out_norm_sg,
           w_out, norm_ffn, peer_w_query, peer_sub_keys, peer_u, peer_v):
    b, s, d = x.shape
    t = b * s
    x2 = x.reshape(t, d)
    row = lambda v: v.reshape(1, -1)

    proj = _inproj(x2, row(norm_mix), w_in.astype(BF16), row(q_norm), row(k_norm), row(sg_norm))
    y_sb = _attn(proj.reshape(b, s, -1)).reshape(t, -1)
    sg_b_bcast = jnp.broadcast_to(sg_b[:, :, None], sg_w.shape)
    x1, h2, h2t = _mix_out(y_sb, proj, sg_w, sg_b_bcast, x2, w_out.astype(BF16),
                           row(out_norm_sb), row(out_norm_sg), row(norm_ffn))

    s2, st5 = _query(h2, peer_w_query.astype(BF16), peer_sub_keys.astype(BF16))
    th, e1, m2 = _topk(st5)
    vt = _transpose_tiles(peer_v, BF16, te=PEER_EXPERT_TILE)
    out = _peer(h2t, peer_u.astype(BF16), vt, s2, m2, th, e1, x1)
    return out.reshape(b, s, d)


def kernel(x, norm_mix, w_in, q_norm, k_norm, sg_norm, sg_w, sg_b, out_norm_sb, out_norm_sg,
           w_out, norm_ffn, peer_w_query, peer_sub_keys, peer_u, peer_v):
    for layer in range(norm_mix.shape[0]):
        x = _layer(x, norm_mix[layer], w_in[layer], q_norm[layer], k_norm[layer], sg_norm[layer],
                   sg_w[layer], sg_b[layer], out_norm_sb[layer], out_norm_sg[layer], w_out[layer],
                   norm_ffn[layer], peer_w_query[layer], peer_sub_keys[layer], peer_u[layer],
                   peer_v[layer])
    return x
```

```python
import functools
import math

import jax
import jax.numpy as jnp
from jax import lax
from jax.experimental import pallas as pl
from jax.experimental.pallas import tpu as pltpu

F32 = jnp.float32
BF16 = jnp.bfloat16

RMS_EPS = 1e-6
LANES = 128
SUBLANES = 8
CHUNK = 64
SB_HEADS = 8
SG_GROUPS = 8
PEER_HEADS = 8
PEER_KEYS = 128
PEER_TOPK = 16
VMEM_LIMIT = 56 * 1024 * 1024

_NT = (((1,), (1,)), ((), ()))
_LOG2E = 1.0 / math.log(2.0)


def _gelu(x):
    return 0.5 * x * (1.0 + lax.erf(x * (1.0 / math.sqrt(2.0))))


def _rms(x):
    return x * lax.rsqrt(jnp.mean(x * x, axis=-1, keepdims=True) + RMS_EPS)


def _inproj_kernel(x_ref, nm_ref, w_ref, qn_ref, kn_ref, sgn_ref, o_ref, h_ref, *, tiles_per_region):
    j = pl.program_id(1)
    tn = o_ref.shape[1]

    @pl.when(j == 0)
    def _():
        h_ref[...] = (_rms(x_ref[...]) * nm_ref[...]).astype(BF16)

    region = j // tiles_per_region
    mxu_n = 2 * LANES

    def sweep(epilogue):
        for c0 in range(0, tn, mxu_n):
            acc = jnp.dot(h_ref[...], w_ref[:, c0:c0 + mxu_n], preferred_element_type=F32)
            for c in range(c0, c0 + mxu_n, LANES):
                sl = slice(c, c + LANES)
                o_ref[:, sl] = epilogue(acc[:, c - c0:c - c0 + LANES], sl).astype(o_ref.dtype)

    @pl.when(region == 0)
    def _():
        sweep(lambda a, sl: _rms(a) * qn_ref[...])

    @pl.when(region == 1)
    def _():
        sweep(lambda a, sl: _rms(a) * kn_ref[...])

    @pl.when(region == 2)
    def _():
        sweep(lambda a, sl: a)

    @pl.when(region == 3)
    def _():
        sweep(lambda a, sl: _gelu(a))

    @pl.when(region == 4)
    def _():
        sweep(lambda a, sl: _rms(_gelu(a)) * sgn_ref[:, sl])


def _inproj(x2, norm_mix, w_in, q_norm, k_norm, sg_norm, *, tm=1024, tn=1024):
    t, d = x2.shape
    n = w_in.shape[1]
    region_w = SB_HEADS * LANES
    tpr = region_w // tn
    return pl.pallas_call(
        functools.partial(_inproj_kernel, tiles_per_region=tpr),
        grid=(t // tm, n // tn),
        in_specs=[
            pl.BlockSpec((tm, d), lambda i, j: (i, 0)),
            pl.BlockSpec((1, d), lambda i, j: (0, 0)),
            pl.BlockSpec((d, tn), lambda i, j: (0, j)),
            pl.BlockSpec((1, LANES), lambda i, j: (0, 0)),
            pl.BlockSpec((1, LANES), lambda i, j: (0, 0)),
            pl.BlockSpec((1, tn), lambda i, j: (0, jnp.maximum(j - 4 * tpr, 0))),
        ],
        out_specs=pl.BlockSpec((tm, tn), lambda i, j: (i, j)),
        out_shape=jax.ShapeDtypeStruct((t, n), BF16),
        scratch_shapes=[pltpu.VMEM((tm, d), BF16)],
        compiler_params=pltpu.CompilerParams(
            dimension_semantics=("parallel", "arbitrary"), vmem_limit_bytes=VMEM_LIMIT),
        name="inproj",
    )(x2, norm_mix, w_in, q_norm, k_norm, sg_norm)


def _attn_kernel(q_ref, k_ref, v_ref, tri_ref, o_ref, acc_ref, c_ref, *, scale):
    i = pl.program_id(1)
    row = lax.broadcasted_iota(jnp.int32, (LANES, LANES), 0)
    col = lax.broadcasted_iota(jnp.int32, (LANES, LANES), 1)
    strict = col < row

    heads = range(SB_HEADS)
    cols = [slice(h * LANES, (h + 1) * LANES) for h in heads]

    def blocks(starts, diag):
        chains = [(start, h) for start in starts for h in heads]
        zs = [lax.dot_general(q_ref[0, :, cols[h]], k_ref[0, pl.ds(start, LANES), cols[h]], _NT,
                              preferred_element_type=F32) * (scale * _LOG2E) for start, h in chains]
        sps = [jnp.maximum(z, 0.0) + jnp.log2(1.0 + jnp.exp2(-jnp.abs(z))) for z in zs]
        if diag:
            sps = [jnp.where(strict, sp, 0.0) for sp in sps]
        css = []
        for sp in sps:
            hi = sp.astype(BF16)
            lo = (sp - hi.astype(F32)).astype(BF16)
            css.append(jnp.dot(jnp.concatenate([hi, lo], axis=1), tri_ref[...], preferred_element_type=F32))
        carried = [None if diag else c_ref[:, cols[h]] for h in heads]
        acc = [None if diag else acc_ref[:, cols[h]] for h in heads]
        for n, (start, h) in enumerate(chains):
            tail = css[n][:, :LANES]
            if carried[h] is not None:
                tail = tail + carried[h]
            w = jnp.exp2(zs[n] - sps[n] - tail)
            if diag:
                w = jnp.where(strict, w, 0.0)
            pv = jnp.dot(w.astype(BF16), v_ref[0, pl.ds(start, LANES), cols[h]], preferred_element_type=F32)
            acc[h] = pv if acc[h] is None else acc[h] + pv
            total = css[n][:, LANES:]
            carried[h] = total if carried[h] is None else carried[h] + total
        for h in heads:
            acc_ref[:, cols[h]] = acc[h]
            c_ref[:, cols[h]] = carried[h]

    blocks([pl.multiple_of(i * LANES, LANES)], True)

    def body(t, carry):
        right = pl.multiple_of((i - 1 - 2 * t) * LANES, LANES)
        blocks([right, pl.multiple_of(right - LANES, LANES)], False)
        return carry

    lax.fori_loop(0, i // 2, body, 0)

    @pl.when(i % 2 == 1)
    def _():
        blocks([0], False)

    o_ref[0] = acc_ref[...].astype(o_ref.dtype)


def _cumsum_matrix():
    j = lax.broadcasted_iota(jnp.int32, (2 * LANES, 2 * LANES), 0) % LANES
    s = lax.broadcasted_iota(jnp.int32, (2 * LANES, 2 * LANES), 1)
    return jnp.where((s >= LANES) | (j > s), 1.0, 0.0).astype(BF16)


def _attn(proj3):
    b, s, _ = proj3.shape
    w = SB_HEADS * LANES
    return pl.pallas_call(
        functools.partial(_attn_kernel, scale=LANES ** -0.5),
        grid=(b, s // LANES),
        in_specs=[
            pl.BlockSpec((1, LANES, w), lambda bi, i: (bi, i, 0)),
            pl.BlockSpec((1, s, w), lambda bi, i: (bi, 0, 1)),
            pl.BlockSpec((1, s, w), lambda bi, i: (bi, 0, 2)),
            pl.BlockSpec((2 * LANES, 2 * LANES), lambda bi, i: (0, 0)),
        ],
        out_specs=pl.BlockSpec((1, LANES, w), lambda bi, i: (bi, i, 0)),
        out_shape=jax.ShapeDtypeStruct((b, s, w), BF16),
        scratch_shapes=[pltpu.VMEM((LANES, w), F32), pltpu.VMEM((LANES, w), F32)],
        compiler_params=pltpu.CompilerParams(
            dimension_semantics=("parallel", "arbitrary"), vmem_limit_bytes=VMEM_LIMIT),
        name="attn",
    )(proj3, proj3, proj3, _cumsum_matrix())


def _mix_out_kernel(ysb_ref, u_ref, vb_ref, sgw_ref, sgb_ref, x_ref, wout_ref, gsb_ref, gsg_ref, nf_ref,
                    x1_ref, h2_ref, h2t_ref, ysg_ref):
    tm = x_ref.shape[0]
    p = lax.broadcasted_iota(jnp.int32, (LANES, LANES), 0) // CHUNK
    q = lax.broadcasted_iota(jnp.int32, (LANES, LANES), 1) // CHUNK
    causal = p >= q
    for g in range(SG_GROUPS):
        w = jnp.where(causal, sgw_ref[g], 0.0).astype(BF16)
        cols = slice(g * LANES, (g + 1) * LANES)
        for r in range(tm // LANES):
            rows = slice(r * LANES, (r + 1) * LANES)
            mixed = jnp.dot(w, vb_ref[rows, cols], preferred_element_type=F32) + sgb_ref[g]
            ysg_ref[rows, cols] = u_ref[rows, cols].astype(F32) * mixed
    sbw = ysb_ref.shape[1]
    y_sb = (_rms(ysb_ref[...].astype(F32)) * gsb_ref[...]).astype(BF16)
    y_sg = (_rms(ysg_ref[...]) * gsg_ref[...]).astype(BF16)
    y = jnp.dot(y_sb, wout_ref[:sbw, :], preferred_element_type=F32)
    y = y + jnp.dot(y_sg, wout_ref[sbw:, :], preferred_element_type=F32)
    x1 = x_ref[...] + y
    x1_ref[...] = x1
    h2 = _rms(x1) * nf_ref[...]
    h2_ref[...] = h2.astype(BF16)
    for c in range(0, h2.shape[1], LANES):
        h2t_ref[c:c + LANES, :] = h2[:, c:c + LANES].T.astype(BF16)


def _mix_out(y_sb, proj, sg_w, sg_b_bcast, x2, w_out, g_sb, g_sg, norm_ffn, *, tm=512):
    t, d = x2.shape
    sbw = y_sb.shape[1]
    sgw = SG_GROUPS * LANES
    u_blk = (3 * sbw) // sgw
    return pl.pallas_call(
        _mix_out_kernel,
        grid=(t // tm,),
        in_specs=[
            pl.BlockSpec((tm, sbw), lambda i: (i, 0)),
            pl.BlockSpec((tm, sgw), lambda i: (i, u_blk)),
            pl.BlockSpec((tm, sgw), lambda i: (i, u_blk + 1)),
            pl.BlockSpec((SG_GROUPS, LANES, LANES), lambda i: (0, 0, 0)),
            pl.BlockSpec((SG_GROUPS, LANES, LANES), lambda i: (0, 0, 0)),
            pl.BlockSpec((tm, d), lambda i: (i, 0)),
            pl.BlockSpec((sbw + sgw, d), lambda i: (0, 0)),
            pl.BlockSpec((1, sbw), lambda i: (0, 0)),
            pl.BlockSpec((1, sgw), lambda i: (0, 0)),
            pl.BlockSpec((1, d), lambda i: (0, 0)),
        ],
        out_specs=[pl.BlockSpec((tm, d), lambda i: (i, 0)), pl.BlockSpec((tm, d), lambda i: (i, 0)),
                   pl.BlockSpec((d, tm), lambda i: (0, i))],
        out_shape=[jax.ShapeDtypeStruct((t, d), F32), jax.ShapeDtypeStruct((t, d), BF16),
                   jax.ShapeDtypeStruct((d, t), BF16)],
        scratch_shapes=[pltpu.VMEM((tm, sgw), F32)],
        compiler_params=pltpu.CompilerParams(
            dimension_semantics=("parallel",), vmem_limit_bytes=VMEM_LIMIT),
        name="mix_out",
    )(y_sb, proj, proj, sg_w, sg_b_bcast, x2, w_out, g_sb, g_sg, norm_ffn)


def _query_kernel(h2_ref, wq_ref, keys_ref, s2_ref, st_ref):
    qf = jnp.dot(h2_ref[...], wq_ref[...], preferred_element_type=F32)
    hw = 2 * PEER_KEYS
    for h in range(PEER_HEADS):
        qn = _rms(qf[:, h * hw:(h + 1) * hw]).astype(BF16)
        for half in range(2):
            qk = qn[:, half * PEER_KEYS:(half + 1) * PEER_KEYS]
            s = lax.dot_general(keys_ref[half], qk, _NT, preferred_element_type=F32)
            if half == 1:
                s2_ref[h] = s
            for c in range(s.shape[1] // LANES):
                st_ref[h, half, :, c, :] = s[:, c * LANES:(c + 1) * LANES]


def _query(h2, w_query, sub_keys, *, tm=SUBLANES * LANES):
    t, d = h2.shape
    return pl.pallas_call(
        _query_kernel,
        grid=(t // tm,),
        in_specs=[
            pl.BlockSpec((tm, d), lambda i: (i, 0)),
            pl.BlockSpec(w_query.shape, lambda i: (0, 0), pipeline_mode=pl.Buffered(1)),
            pl.BlockSpec(sub_keys.shape, lambda i: (0, 0, 0)),
        ],
        out_specs=[
            pl.BlockSpec((PEER_HEADS, PEER_KEYS, tm), lambda i: (0, 0, i)),
            pl.BlockSpec((PEER_HEADS, 2, PEER_KEYS, tm // LANES, LANES), lambda i: (0, 0, 0, i, 0)),
        ],
        out_shape=[
            jax.ShapeDtypeStruct((PEER_HEADS, PEER_KEYS, t), F32),
            jax.ShapeDtypeStruct((PEER_HEADS, 2, PEER_KEYS, t // LANES, LANES), F32),
        ],
        compiler_params=pltpu.CompilerParams(
            dimension_semantics=("parallel",), vmem_limit_bytes=VMEM_LIMIT),
        name="query",
    )(h2, w_query, sub_keys)


def _sort_pairs(n):
    pairs = []
    p = 1
    while p < n:
        k = p
        while k >= 1:
            for j in range(k % p, n - k, 2 * k):
                for i in range(min(k, n - j - k)):
                    if (i + j) // (2 * p) == (i + j + k) // (2 * p):
                        pairs.append((i + j, i + j + k))
            k //= 2
        p *= 2
    return pairs


def _exchange(v, lo, hi):
    a, b = v[lo], v[hi]
    if b is None:
        return
    if a is None:
        v[lo], v[hi] = b, None
        return
    v[lo], v[hi] = jnp.maximum(a, b), jnp.minimum(a, b)


def _sort_desc(v):
    v = list(v)
    for lo, hi in _sort_pairs(len(v)):
        _exchange(v, lo, hi)
    return v


def _merge_top(x, y):
    n = len(x)
    v = [jnp.maximum(x[i], y[n - 1 - i]) for i in range(n)]
    k = n // 2
    while k >= 1:
        for i in range(n):
            if i & k == 0:
                _exchange(v, i, i + k)
        k //= 2
    return v


def _top_desc(vals, k):
    groups = [_sort_desc(vals[g:g + k]) for g in range(0, len(vals), k)]
    while len(groups) > 1:
        groups = [_merge_top(groups[g], groups[g + 1]) for g in range(0, len(groups), 2)]
    return groups[0]


def _topk_kernel(s_ref, th_ref, e1_ref, m2_ref):
    k = PEER_TOPK
    a = [s_ref[0, 0, n] for n in range(PEER_KEYS)]
    b = [s_ref[0, 1, n] for n in range(PEER_KEYS)]
    ta = _top_desc(a, k)
    tb = _top_desc(b, k)
    cand = [ta[p] + tb[q] for p in range(k) for q in range(k) if (p + 1) * (q + 1) <= k]
    n_sort = 1 << (len(cand) - 1).bit_length()
    tau = _sort_desc(cand + [None] * (n_sort - len(cand)))[k - 1]
    top = ta[0] + tb[0]
    z = jnp.zeros_like(top)
    for c in cand:
        z = z + jnp.where(c >= tau, jnp.exp(c - top), 0.0)
    half_inv_z = 0.5 / z
    m2_ref[0] = tb[0]
    for n in range(PEER_KEYS):
        th = jnp.full_like(top, jnp.inf)
        for q in range(k):
            th = jnp.where(a[n] + tb[q] >= tau, tb[q], th)
        th_ref[0, n] = th
        e1_ref[0, n] = jnp.exp(a[n] - ta[0]) * half_inv_z


def _topk(st5):
    hh, _, nk, nc, _ = st5.shape
    blk = (1, nk, SUBLANES, LANES)
    return pl.pallas_call(
        _topk_kernel,
        grid=(hh, nc // SUBLANES),
        in_specs=[pl.BlockSpec((1, 2, nk, SUBLANES, LANES), lambda h, u: (h, 0, 0, u, 0))],
        out_specs=[
            pl.BlockSpec(blk, lambda h, u: (h, 0, u, 0)),
            pl.BlockSpec(blk, lambda h, u: (h, 0, u, 0)),
            pl.BlockSpec((1, SUBLANES, LANES), lambda h, u: (h, u, 0)),
        ],
        out_shape=[
            jax.ShapeDtypeStruct((hh, nk, nc, LANES), F32),
            jax.ShapeDtypeStruct((hh, nk, nc, LANES), F32),
            jax.ShapeDtypeStruct((hh, nc, LANES), F32),
        ],
        compiler_params=pltpu.CompilerParams(
            dimension_semantics=("parallel", "parallel"), vmem_limit_bytes=VMEM_LIMIT),
        name="topk",
    )(st5)


_PACK_ROWS = 16
PEER_EXPERT_TILE = 512


def _peer_kernel(h2t_ref, u_ref, vt_ref, s2_ref, m2_ref, th_ref, e1_ref, x1_ref, o_ref,
                 st0_ref, st1_ref, w0_ref, w1_ref, e2_ref, acc_ref, *, nj, n_tiles):
    s = pl.program_id(0)
    st_ref = (st0_ref, st1_ref)
    w_ref = (w0_ref, w1_ref)
    te, tm = w0_ref.shape
    n_chunks = tm // LANES
    n_l = te // PEER_KEYS
    heads = range(PEER_HEADS)
    sb = jnp.clip(s - 1, 0, n_tiles - 1)
    sc = jnp.clip(s - 2, 0, n_tiles - 1)
    ib, jb = sb // nj, sb % nj
    chunk0 = (ib % (SUBLANES // n_chunks)) * n_chunks

    @pl.when(s == 0)
    def _():
        st1_ref[...] = jnp.zeros_like(st1_ref)
        w0_ref[...] = jnp.zeros_like(w0_ref)

    @pl.when(jb == 0)
    def _():
        for h in heads:
            for c in range(n_chunks):
                lanes = slice(c * LANES, (c + 1) * LANES)
                e2_ref[h, :, lanes] = jnp.exp(s2_ref[h, :, lanes] - m2_ref[h, pl.ds(chunk0 + c, 1), :])

    @pl.when(sc % nj == 0)
    def _():
        acc_ref[...] = jnp.zeros_like(acc_ref)

    d = acc_ref.shape[0]
    halves = (0, 1)

    def step(par):
        def scores(mh, nt):
            m = slice(mh * (te // 2), (mh + 1) * (te // 2))
            n = slice(nt * (tm // 2), (nt + 1) * (tm // 2))
            st_ref[par][m, n] = jnp.dot(u_ref[m, :], h2t_ref[:, n], preferred_element_type=F32)

        def accumulate(mq, nt):
            m = slice(mq * (d // 8), (mq + 1) * (d // 8))
            n = slice(nt * (tm // 2), (nt + 1) * (tm // 2))
            acc_ref[m, n] += jnp.dot(vt_ref[m, :], w_ref[par][:, n], preferred_element_type=F32)

        def gate_rows(c, r):
            lanes = slice(c * LANES, (c + 1) * LANES)
            keys = slice(r * _PACK_ROWS, (r + 1) * _PACK_ROWS)
            gates = [None] * n_l
            for h in heads:
                s2 = s2_ref[h, keys, lanes]
                e2 = e2_ref[h, keys, lanes]
                for l in range(n_l):
                    i1 = jb * n_l + l
                    th = th_ref[h, i1, pl.ds(chunk0 + c, 1), :]
                    e1 = e1_ref[h, i1, pl.ds(chunk0 + c, 1), :]
                    term = jnp.where(s2 >= th, e2, 0.0) * e1
                    gates[l] = term if gates[l] is None else gates[l] + term
            for l in range(n_l):
                rows = slice(l * PEER_KEYS + r * _PACK_ROWS, l * PEER_KEYS + (r + 1) * _PACK_ROWS)
                a = st_ref[1 - par][rows, lanes]
                w_ref[1 - par][rows, lanes] = (
                    a * (1.0 + lax.erf(a * (1.0 / math.sqrt(2.0)))) * gates[l]).astype(BF16)

        mxu_work = []
        for nt in halves:
            for mh in halves:
                mxu_work += [functools.partial(scores, mh, nt)]
                mxu_work += [functools.partial(accumulate, 4 * mh + q, nt) for q in range(4)]
        vpu_work = [functools.partial(gate_rows, c, r) for r in range(PEER_KEYS // _PACK_ROWS)
                    for c in range(n_chunks)]
        done = 0
        for k, mm in enumerate(mxu_work):
            mm()
            upto = (k + 1) * len(vpu_work) // len(mxu_work)
            for blk in vpu_work[done:upto]:
                blk()
            done = upto

    @pl.when(s % 2 == 0)
    def _():
        step(0)

    @pl.when(s % 2 == 1)
    def _():
        step(1)

    @pl.when((s >= 2) & (sc % nj == nj - 1))
    def _():
        for n in range(tm // LANES):
            rows = slice(n * LANES, (n + 1) * LANES)
            for m in range(d // LANES):
                cols = slice(m * LANES, (m + 1) * LANES)
                o_ref[rows, cols] = x1_ref[rows, cols] + acc_ref[cols, rows].T


def _peer(h2t, u_bf, vt_bf, s2, m2, th, e1, x1, *, tm=512):
    d, t = h2t.shape
    nk = PEER_KEYS
    nj, _, te = vt_bf.shape
    n_tiles = (t // tm) * nj
    tile_a = lambda s: jnp.minimum(s, n_tiles - 1)
    tile_b = lambda s: jnp.clip(s - 1, 0, n_tiles - 1)
    tile_c = lambda s: jnp.clip(s - 2, 0, n_tiles - 1)
    big = lambda s: (tile_b(s) // nj) * tm // (SUBLANES * LANES)
    return pl.pallas_call(
        functools.partial(_peer_kernel, nj=nj, n_tiles=n_tiles),
        grid=(n_tiles + 2,),
        in_specs=[
            pl.BlockSpec((d, tm), lambda s: (0, tile_a(s) // nj)),
            pl.BlockSpec((te, d), lambda s: (tile_a(s) % nj, 0)),
            pl.BlockSpec((None, d, te), lambda s: (tile_c(s) % nj, 0, 0)),
            pl.BlockSpec((PEER_HEADS, nk, tm), lambda s: (0, 0, tile_b(s) // nj)),
            pl.BlockSpec((PEER_HEADS, SUBLANES, LANES), lambda s: (0, big(s), 0)),
            pl.BlockSpec((PEER_HEADS, nk, SUBLANES, LANES), lambda s: (0, 0, big(s), 0)),
            pl.BlockSpec((PEER_HEADS, nk, SUBLANES, LANES), lambda s: (0, 0, big(s), 0)),
            pl.BlockSpec((tm, d), lambda s: (tile_c(s) // nj, 0), pipeline_mode=pl.Buffered(1)),
        ],
        out_specs=pl.BlockSpec((tm, d), lambda s: (tile_c(s) // nj, 0)),
        out_shape=jax.ShapeDtypeStruct((t, d), F32),
        scratch_shapes=[pltpu.VMEM((te, tm), F32), pltpu.VMEM((te, tm), F32),
                        pltpu.VMEM((te, tm), BF16), pltpu.VMEM((te, tm), BF16),
                        pltpu.VMEM((PEER_HEADS, nk, tm), F32), pltpu.VMEM((d, tm), F32)],
        compiler_params=pltpu.CompilerParams(
            dimension_semantics=("arbitrary",), vmem_limit_bytes=VMEM_LIMIT),
        name="peer",
    )(h2t, u_bf, vt_bf, s2, m2, th, e1, x1)


def _transpose_tiles_kernel(x_ref, o_ref):
    blk = o_ref.shape[1]
    for c0 in range(0, o_ref.shape[0], blk):
        o_ref[c0:c0 + blk, :] = x_ref[:, c0:c0 + blk].T.astype(o_ref.dtype)


def _transpose_tiles(x, dtype, *, te):
    r, c = x.shape
    return pl.pallas_call(
        _transpose_tiles_kernel,
        grid=(r // te,),
        in_specs=[pl.BlockSpec((te, c), lambda j: (j, 0))],
        out_specs=pl.BlockSpec((None, c, te), lambda j: (j, 0, 0)),
        out_shape=jax.ShapeDtypeStruct((r // te, c, te), dtype),
        compiler_params=pltpu.CompilerParams(
            dimension_semantics=("parallel",), vmem_limit_bytes=VMEM_LIMIT),
        name="transpose_tiles",
    )(x)


def _layer(x, norm_mix, w_in, q_norm, k_norm, sg_norm, sg_w, sg_b, out_norm_sb, out_norm_sg,
           w_out, norm_ffn, peer_w_query, peer_sub_keys, peer_u, peer_v):
    b, s, d = x.shape
    t = b * s
    x2 = x.reshape(t, d)
    row = lambda v: v.reshape(1, -1)

    proj = _inproj(x2, row(norm_mix), w_in.astype(BF16), row(q_norm), row(k_norm), row(sg_norm))
    y_sb = _attn(proj.reshape(b, s, -1)).reshape(t, -1)
    sg_b_bcast = jnp.broadcast_to(sg_b[:, :, None], sg_w.shape)
    x1, h2, h2t = _mix_out(y_sb, proj, sg_w, sg_b_bcast, x2, w_out.astype(BF16),
                           row(out_norm_sb), row(out_norm_sg), row(norm_ffn))

    s2, st5 = _query(h2, peer_w_query.astype(BF16), peer_sub_keys.astype(BF16))
    th, e1, m2 = _topk(st5)
    vt = _transpose_tiles(peer_v, BF16, te=PEER_EXPERT_TILE)
    out = _peer(h2t, peer_u.astype(BF16), vt, s2, m2, th, e1, x1)
    return out.reshape(b, s, d)


def kernel(x, norm_mix, w_in, q_norm, k_norm, sg_norm, sg_w, sg_b, out_norm_sb, out_norm_sg,
           w_out, norm_ffn, peer_w_query, peer_sub_keys, peer_u, peer_v):
    for layer in range(norm_mix.shape[0]):
        x = _layer(x, norm_mix[layer], w_in[layer], q_norm[layer], k_norm[layer], sg_norm[layer],
                   sg_w[layer], sg_b[layer], out_norm_sb[layer], out_norm_sg[layer], w_out[layer],
                   norm_ffn[layer], peer_w_query[layer], peer_sub_keys[layer], peer_u[layer],
                   peer_v[layer])
    return x
```

```python
import functools
import math

import jax
import jax.numpy as jnp
from jax import lax
from jax.experimental import pallas as pl
from jax.experimental.pallas import tpu as pltpu

F32 = jnp.float32
BF16 = jnp.bfloat16

RMS_EPS = 1e-6
LANES = 128
SUBLANES = 8
CHUNK = 64
SB_HEADS = 8
SG_GROUPS = 8
PEER_HEADS = 8
PEER_KEYS = 128
PEER_TOPK = 16
VMEM_LIMIT = 56 * 1024 * 1024

_NT = (((1,), (1,)), ((), ()))
_LOG2E = 1.0 / math.log(2.0)


def _gelu(x):
    return 0.5 * x * (1.0 + lax.erf(x * (1.0 / math.sqrt(2.0))))


def _rms(x):
    return x * lax.rsqrt(jnp.mean(x * x, axis=-1, keepdims=True) + RMS_EPS)


def _inproj_kernel(x_ref, nm_ref, w_ref, qn_ref, kn_ref, sgn_ref, o_ref, h_ref, *, tiles_per_region):
    j = pl.program_id(1)
    tn = o_ref.shape[1]

    @pl.when(j == 0)
    def _():
        h_ref[...] = (_rms(x_ref[...]) * nm_ref[...]).astype(BF16)

    region = j // tiles_per_region
    mxu_n = 2 * LANES

    def sweep(epilogue):
        for c0 in range(0, tn, mxu_n):
            acc = jnp.dot(h_ref[...], w_ref[:, c0:c0 + mxu_n], preferred_element_type=F32)
            for c in range(c0, c0 + mxu_n, LANES):
                sl = slice(c, c + LANES)
                o_ref[:, sl] = epilogue(acc[:, c - c0:c - c0 + LANES], sl).astype(o_ref.dtype)

    @pl.when(region == 0)
    def _():
        sweep(lambda a, sl: _rms(a) * qn_ref[...])

    @pl.when(region == 1)
    def _():
        sweep(lambda a, sl: _rms(a) * kn_ref[...])

    @pl.when(region == 2)
    def _():
        sweep(lambda a, sl: a)

    @pl.when(region == 3)
    def _():
        sweep(lambda a, sl: _gelu(a))

    @pl.when(region == 4)
    def _():
        sweep(lambda a, sl: _rms(_gelu(a)) * sgn_ref[:, sl])


def _inproj(x2, norm_mix, w_in, q_norm, k_norm, sg_norm, *, tm=1024, tn=1024):
    t, d = x2.shape
    n = w_in.shape[1]
    region_w = SB_HEADS * LANES
    tpr = region_w // tn
    return pl.pallas_call(
        functools.partial(_inproj_kernel, tiles_per_region=tpr),
        grid=(t // tm, n // tn),
        in_specs=[
            pl.BlockSpec((tm, d), lambda i, j: (i, 0)),
            pl.BlockSpec((1, d), lambda i, j: (0, 0)),
            pl.BlockSpec((d, tn), lambda i, j: (0, j)),
            pl.BlockSpec((1, LANES), lambda i, j: (0, 0)),
            pl.BlockSpec((1, LANES), lambda i, j: (0, 0)),
            pl.BlockSpec((1, tn), lambda i, j: (0, jnp.maximum(j - 4 * tpr, 0))),
        ],
        out_specs=pl.BlockSpec((tm, tn), lambda i, j: (i, j)),
        out_shape=jax.ShapeDtypeStruct((t, n), BF16),
        scratch_shapes=[pltpu.VMEM((tm, d), BF16)],
        compiler_params=pltpu.CompilerParams(
            dimension_semantics=("parallel", "arbitrary"), vmem_limit_bytes=VMEM_LIMIT),
        name="inproj",
    )(x2, norm_mix, w_in, q_norm, k_norm, sg_norm)


def _attn_kernel(q_ref, k_ref, v_ref, tri_ref, o_ref, acc_ref, c_ref, *, scale):
    i = pl.program_id(1)
    row = lax.broadcasted_iota(jnp.int32, (LANES, LANES), 0)
    col = lax.broadcasted_iota(jnp.int32, (LANES, LANES), 1)
    strict = col < row

    heads = range(SB_HEADS)
    cols = [slice(h * LANES, (h + 1) * LANES) for h in heads]

    def blocks(starts, diag):
        chains = [(start, h) for start in starts for h in heads]
        zs = [lax.dot_general(q_ref[0, :, cols[h]], k_ref[0, pl.ds(start, LANES), cols[h]], _NT,
                              preferred_element_type=F32) * (scale * _LOG2E) for start, h in chains]
        sps = [jnp.maximum(z, 0.0) + jnp.log2(1.0 + jnp.exp2(-jnp.abs(z))) for z in zs]
        if diag:
            sps = [jnp.where(strict, sp, 0.0) for sp in sps]
        css = []
        for sp in sps:
            hi = sp.astype(BF16)
            lo = (sp - hi.astype(F32)).astype(BF16)
            css.append(jnp.dot(jnp.concatenate([hi, lo], axis=1), tri_ref[...], preferred_element_type=F32))
        carried = [None if diag else c_ref[:, cols[h]] for h in heads]
        acc = [None if diag else acc_ref[:, cols[h]] for h in heads]
        for n, (start, h) in enumerate(chains):
            tail = css[n][:, :LANES]
            if carried[h] is not None:
                tail = tail + carried[h]
            w = jnp.exp2(zs[n] - sps[n] - tail)
            if diag:
                w = jnp.where(strict, w, 0.0)
            pv = jnp.dot(w.astype(BF16), v_ref[0, pl.ds(start, LANES), cols[h]], preferred_element_type=F32)
            acc[h] = pv if acc[h] is None else acc[h] + pv
            total = css[n][:, LANES:]
            carried[h] = total if carried[h] is None else carried[h] + total
        for h in heads:
            acc_ref[:, cols[h]] = acc[h]
            c_ref[:, cols[h]] = carried[h]

    blocks([pl.multiple_of(i * LANES, LANES)], True)

    def body(t, carry):
        right = pl.multiple_of((i - 1 - 2 * t) * LANES, LANES)
        blocks([right, pl.multiple_of(right - LANES, LANES)], False)
        return carry

    lax.fori_loop(0, i // 2, body, 0)

    @pl.when(i % 2 == 1)
    def _():
        blocks([0], False)

    o_ref[0] = acc_ref[...].astype(o_ref.dtype)


def _cumsum_matrix():
    j = lax.broadcasted_iota(jnp.int32, (2 * LANES, 2 * LANES), 0) % LANES
    s = lax.broadcasted_iota(jnp.int32, (2 * LANES, 2 * LANES), 1)
    return jnp.where((s >= LANES) | (j > s), 1.0, 0.0).astype(BF16)


def _attn(proj3):
    b, s, _ = proj3.shape
    w = SB_HEADS * LANES
    return pl.pallas_call(
        functools.partial(_attn_kernel, scale=LANES ** -0.5),
        grid=(b, s // LANES),
        in_specs=[
            pl.BlockSpec((1, LANES, w), lambda bi, i: (bi, i, 0)),
            pl.BlockSpec((1, s, w), lambda bi, i: (bi, 0, 1)),
            pl.BlockSpec((1, s, w), lambda bi, i: (bi, 0, 2)),
            pl.BlockSpec((2 * LANES, 2 * LANES), lambda bi, i: (0, 0)),
        ],
        out_specs=pl.BlockSpec((1, LANES, w), lambda bi, i: (bi, i, 0)),
        out_shape=jax.ShapeDtypeStruct((b, s, w), BF16),
        scratch_shapes=[pltpu.VMEM((LANES, w), F32), pltpu.VMEM((LANES, w), F32)],
        compiler_params=pltpu.CompilerParams(
            dimension_semantics=("parallel", "arbitrary"), vmem_limit_bytes=VMEM_LIMIT),
        name="attn",
    )(proj3, proj3, proj3, _cumsum_matrix())


def _mix_out_kernel(ysb_ref, u_ref, vb_ref, sgw_ref, sgb_ref, x_ref, wout_ref, gsb_ref, gsg_ref, nf_ref,
                    x1_ref, h2_ref, h2t_ref, ysg_ref):
    tm = x_ref.shape[0]
    p = lax.broadcasted_iota(jnp.int32, (LANES, LANES), 0) // CHUNK
    q = lax.broadcasted_iota(jnp.int32, (LANES, LANES), 1) // CHUNK
    causal = p >= q
    for g in range(SG_GROUPS):
        w = jnp.where(causal, sgw_ref[g], 0.0).astype(BF16)
        cols = slice(g * LANES, (g + 1) * LANES)
        for r in range(tm // LANES):
            rows = slice(r * LANES, (r + 1) * LANES)
            mixed = jnp.dot(w, vb_ref[rows, cols], preferred_element_type=F32) + sgb_ref[g]
            ysg_ref[rows, cols] = u_ref[rows, cols].astype(F32) * mixed
    sbw = ysb_ref.shape[1]
    y_sb = (_rms(ysb_ref[...].astype(F32)) * gsb_ref[...]).astype(BF16)
    y_sg = (_rms(ysg_ref[...]) * gsg_ref[...]).astype(BF16)
    y = jnp.dot(y_sb, wout_ref[:sbw, :], preferred_element_type=F32)
    y = y + jnp.dot(y_sg, wout_ref[sbw:, :], preferred_element_type=F32)
    x1 = x_ref[...] + y
    x1_ref[...] = x1
    h2 = _rms(x1) * nf_ref[...]
    h2_ref[...] = h2.astype(BF16)
    for c in range(0, h2.shape[1], LANES):
        h2t_ref[c:c + LANES, :] = h2[:, c:c + LANES].T.astype(BF16)


def _mix_out(y_sb, proj, sg_w, sg_b_bcast, x2, w_out, g_sb, g_sg, norm_ffn, *, tm=512):
    t, d = x2.shape
    sbw = y_sb.shape[1]
    sgw = SG_GROUPS * LANES
    u_blk = (3 * sbw) // sgw
    return pl.pallas_call(
        _mix_out_kernel,
        grid=(t // tm,),
        in_specs=[
            pl.BlockSpec((tm, sbw), lambda i: (i, 0)),
            pl.BlockSpec((tm, sgw), lambda i: (i, u_blk)),
            pl.BlockSpec((tm, sgw), lambda i: (i, u_blk + 1)),
            pl.BlockSpec((SG_GROUPS, LANES, LANES), lambda i: (0, 0, 0)),
            pl.BlockSpec((SG_GROUPS, LANES, LANES), lambda i: (0, 0, 0)),
            pl.BlockSpec((tm, d), lambda i: (i, 0)),
            pl.BlockSpec((sbw + sgw, d), lambda i: (0, 0)),
            pl.BlockSpec((1, sbw), lambda i: (0, 0)),
            pl.BlockSpec((1, sgw), lambda i: (0, 0)),
            pl.BlockSpec((1, d), lambda i: (0, 0)),
        ],
        out_specs=[pl.BlockSpec((tm, d), lambda i: (i, 0)), pl.BlockSpec((tm, d), lambda i: (i, 0)),
                   pl.BlockSpec((d, tm), lambda i: (0, i))],
        out_shape=[jax.ShapeDtypeStruct((t, d), F32), jax.ShapeDtypeStruct((t, d), BF16),
                   jax.ShapeDtypeStruct((d, t), BF16)],
        scratch_shapes=[pltpu.VMEM((tm, sgw), F32)],
        compiler_params=pltpu.CompilerParams(
            dimension_semantics=("parallel",), vmem_limit_bytes=VMEM_LIMIT),
        name="mix_out",
    )(y_sb, proj, proj, sg_w, sg_b_bcast, x2, w_out, g_sb, g_sg, norm_ffn)


def _query_kernel(h2_ref, wq_ref, keys_ref, s2_ref, st_ref):
    qf = jnp.dot(h2_ref[...], wq_ref[...], preferred_element_type=F32)
    hw = 2 * PEER_KEYS
    for h in range(PEER_HEADS):
        qn = _rms(qf[:, h * hw:(h + 1) * hw]).astype(BF16)
        for half in range(2):
            qk = qn[:, half * PEER_KEYS:(half + 1) * PEER_KEYS]
            s = lax.dot_general(keys_ref[half], qk, _NT, preferred_element_type=F32)
            if half == 1:
                s2_ref[h] = s
            for c in range(s.shape[1] // LANES):
                st_ref[h, half, :, c, :] = s[:, c * LANES:(c + 1) * LANES]


def _query(h2, w_query, sub_keys, *, tm=SUBLANES * LANES):
    t, d = h2.shape
    return pl.pallas_call(
        _query_kernel,
        grid=(t // tm,),
        in_specs=[
            pl.BlockSpec((tm, d), lambda i: (i, 0)),
            pl.BlockSpec(w_query.shape, lambda i: (0, 0), pipeline_mode=pl.Buffered(1)),
            pl.BlockSpec(sub_keys.shape, lambda i: (0, 0, 0)),
        ],
        out_specs=[
            pl.BlockSpec((PEER_HEADS, PEER_KEYS, tm), lambda i: (0, 0, i)),
            pl.BlockSpec((PEER_HEADS, 2, PEER_KEYS, tm // LANES, LANES), lambda i: (0, 0, 0, i, 0)),
        ],
        out_shape=[
            jax.ShapeDtypeStruct((PEER_HEADS, PEER_KEYS, t), F32),
            jax.ShapeDtypeStruct((PEER_HEADS, 2, PEER_KEYS, t // LANES, LANES), F32),
        ],
        compiler_params=pltpu.CompilerParams(
            dimension_semantics=("parallel",), vmem_limit_bytes=VMEM_LIMIT),
        name="query",
    )(h2, w_query, sub_keys)


def _sort_pairs(n):
    pairs = []
    p = 1
    while p < n:
        k = p
        while k >= 1:
            for j in range(k % p, n - k, 2 * k):
                for i in range(min(k, n - j - k)):
                    if (i + j) // (2 * p) == (i + j + k) // (2 * p):
                        pairs.append((i + j, i + j + k))
            k //= 2
        p *= 2
    return pairs


def _exchange(v, lo, hi):
    a, b = v[lo], v[hi]
    if b is None:
        return
    if a is None:
        v[lo], v[hi] = b, None
        return
    v[lo], v[hi] = jnp.maximum(a, b), jnp.minimum(a, b)


def _sort_desc(v):
    v = list(v)
    for lo, hi in _sort_pairs(len(v)):
        _exchange(v, lo, hi)
    return v


def _merge_top(x, y):
    n = len(x)
    v = [jnp.maximum(x[i], y[n - 1 - i]) for i in range(n)]
    k = n // 2
    while k >= 1:
        for i in range(n):
            if i & k == 0:
                _exchange(v, i, i + k)
        k //= 2
    return v


def _top_desc(vals, k):
    groups = [_sort_desc(vals[g:g + k]) for g in range(0, len(vals), k)]
    while len(groups) > 1:
        groups = [_merge_top(groups[g], groups[g + 1]) for g in range(0, len(groups), 2)]
    return groups[0]


def _topk_kernel(s_ref, th_ref, e1_ref, m2_ref):
    k = PEER_TOPK
    a = [s_ref[0, 0, n] for n in range(PEER_KEYS)]
    b = [s_ref[0, 1, n] for n in range(PEER_KEYS)]
    ta = _top_desc(a, k)
    tb = _top_desc(b, k)
    cand = [ta[p] + tb[q] for p in range(k) for q in range(k) if (p + 1) * (q + 1) <= k]
    n_sort = 1 << (len(cand) - 1).bit_length()
    tau = _sort_desc(cand + [None] * (n_sort - len(cand)))[k - 1]
    top = ta[0] + tb[0]
    z = jnp.zeros_like(top)
    for c in cand:
        z = z + jnp.where(c >= tau, jnp.exp(c - top), 0.0)
    half_inv_z = 0.5 / z
    m2_ref[0] = tb[0]
    for n in range(PEER_KEYS):
        th = jnp.full_like(top, jnp.inf)
        for q in range(k):
            th = jnp.where(a[n] + tb[q] >= tau, tb[q], th)
        th_ref[0, n] = th
        e1_ref[0, n] = jnp.exp(a[n] - ta[0]) * half_inv_z


def _topk(st5):
    hh, _, nk, nc, _ = st5.shape
    blk = (1, nk, SUBLANES, LANES)
    return pl.pallas_call(
        _topk_kernel,
        grid=(hh, nc // SUBLANES),
        in_specs=[pl.BlockSpec((1, 2, nk, SUBLANES, LANES), lambda h, u: (h, 0, 0, u, 0))],
        out_specs=[
            pl.BlockSpec(blk, lambda h, u: (h, 0, u, 0)),
            pl.BlockSpec(blk, lambda h, u: (h, 0, u, 0)),
            pl.BlockSpec((1, SUBLANES, LANES), lambda h, u: (h, u, 0)),
        ],
        out_shape=[
            jax.ShapeDtypeStruct((hh, nk, nc, LANES), F32),
            jax.ShapeDtypeStruct((hh, nk, nc, LANES), F32),
            jax.ShapeDtypeStruct((hh, nc, LANES), F32),
        ],
        compiler_params=pltpu.CompilerParams(
            dimension_semantics=("parallel", "parallel"), vmem_limit_bytes=VMEM_LIMIT),
        name="topk",
    )(st5)


_PACK_ROWS = 16
PEER_EXPERT_TILE = 512


def _peer_kernel(h2t_ref, u_ref, vt_ref, s2_ref, m2_ref, th_ref, e1_ref, x1_ref, o_ref,
                 st0_ref, st1_ref, w0_ref, w1_ref, e2_ref, acc_ref, *, nj, n_tiles):
    s = pl.program_id(0)
    st_ref = (st0_ref, st1_ref)
    w_ref = (w0_ref, w1_ref)
    te, tm = w0_ref.shape
    n_chunks = tm // LANES
    n_l = te // PEER_KEYS
    heads = range(PEER_HEADS)
    sb = jnp.clip(s - 1, 0, n_tiles - 1)
    sc = jnp.clip(s - 2, 0, n_tiles - 1)
    ib, jb = sb // nj, sb % nj
    chunk0 = (ib % (SUBLANES // n_chunks)) * n_chunks

    @pl.when(s == 0)
    def _():
        st1_ref[...] = jnp.zeros_like(st1_ref)
        w0_ref[...] = jnp.zeros_like(w0_ref)

    @pl.when(jb == 0)
    def _():
        for h in heads:
            for c in range(n_chunks):
                lanes = slice(c * LANES, (c + 1) * LANES)
                e2_ref[h, :, lanes] = jnp.exp(s2_ref[h, :, lanes] - m2_ref[h, pl.ds(chunk0 + c, 1), :])

    @pl.when(sc % nj == 0)
    def _():
        acc_ref[...] = jnp.zeros_like(acc_ref)

    d = acc_ref.shape[0]
    halves = (0, 1)

    def step(par):
        def scores(mh, nt):
            m = slice(mh * (te // 2), (mh + 1) * (te // 2))
            n = slice(nt * (tm // 2), (nt + 1) * (tm // 2))
            st_ref[par][m, n] = jnp.dot(u_ref[m, :], h2t_ref[:, n], preferred_element_type=F32)

        def accumulate(mq, nt):
            m = slice(mq * (d // 8), (mq + 1) * (d // 8))
            n = slice(nt * (tm // 2), (nt + 1) * (tm // 2))
            acc_ref[m, n] += jnp.dot(vt_ref[m, :], w_ref[par][:, n], preferred_element_type=F32)

        def gate_rows(c, r):
            lanes = slice(c * LANES, (c + 1) * LANES)
            keys = slice(r * _PACK_ROWS, (r + 1) * _PACK_ROWS)
            gates = [None] * n_l
            for h in heads:
                s2 = s2_ref[h, keys, lanes]
                e2 = e2_ref[h, keys, lanes]
                for l in range(n_l):
                    i1 = jb * n_l + l
                    th = th_ref[h, i1, pl.ds(chunk0 + c, 1), :]
                    e1 = e1_ref[h, i1, pl.ds(chunk0 + c, 1), :]
                    term = jnp.where(s2 >= th, e2, 0.0) * e1
                    gates[l] = term if gates[l] is None else gates[l] + term
            for l in range(n_l):
                rows = slice(l * PEER_KEYS + r * _PACK_ROWS, l * PEER_KEYS + (r + 1) * _PACK_ROWS)
                a = st_ref[1 - par][rows, lanes]
                w_ref[1 - par][rows, lanes] = (
                    a * (1.0 + lax.erf(a * (1.0 / math.sqrt(2.0)))) * gates[l]).astype(BF16)

        mxu_work = []
        for nt in halves:
            for mh in halves:
                mxu_work += [(functools.partial(scores, mh, nt), 0)]
                mxu_work += [(functools.partial(accumulate, 4 * mh + q, nt), 1) for q in range(4)]
        vpu_work = [functools.partial(gate_rows, c, r) for r in range(PEER_KEYS // _PACK_ROWS)
                    for c in range(n_chunks)]
        total = sum(share for _, share in mxu_work)
        done = issued = 0
        for mm, share in mxu_work:
            mm()
            issued += share
            upto = issued * len(vpu_work) // total
            for blk in vpu_work[done:upto]:
                blk()
            done = upto

    @pl.when(s % 2 == 0)
    def _():
        step(0)

    @pl.when(s % 2 == 1)
    def _():
        step(1)

    @pl.when((s >= 2) & (sc % nj == nj - 1))
    def _():
        for n in range(tm // LANES):
            rows = slice(n * LANES, (n + 1) * LANES)
            for m in range(d // LANES):
                cols = slice(m * LANES, (m + 1) * LANES)
                o_ref[rows, cols] = x1_ref[rows, cols] + acc_ref[cols, rows].T


def _peer(h2t, u_bf, vt_bf, s2, m2, th, e1, x1, *, tm=512):
    d, t = h2t.shape
    nk = PEER_KEYS
    nj, _, te = vt_bf.shape
    n_tiles = (t // tm) * nj
    tile_a = lambda s: jnp.minimum(s, n_tiles - 1)
    tile_b = lambda s: jnp.clip(s - 1, 0, n_tiles - 1)
    tile_c = lambda s: jnp.clip(s - 2, 0, n_tiles - 1)
    big = lambda s: (tile_b(s) // nj) * tm // (SUBLANES * LANES)
    return pl.pallas_call(
        functools.partial(_peer_kernel, nj=nj, n_tiles=n_tiles),
        grid=(n_tiles + 2,),
        in_specs=[
            pl.BlockSpec((d, tm), lambda s: (0, tile_a(s) // nj)),
            pl.BlockSpec((te, d), lambda s: (tile_a(s) % nj, 0)),
            pl.BlockSpec((None, d, te), lambda s: (tile_c(s) % nj, 0, 0)),
            pl.BlockSpec((PEER_HEADS, nk, tm), lambda s: (0, 0, tile_b(s) // nj)),
            pl.BlockSpec((PEER_HEADS, SUBLANES, LANES), lambda s: (0, big(s), 0)),
            pl.BlockSpec((PEER_HEADS, nk, SUBLANES, LANES), lambda s: (0, 0, big(s), 0)),
            pl.BlockSpec((PEER_HEADS, nk, SUBLANES, LANES), lambda s: (0, 0, big(s), 0)),
            pl.BlockSpec((tm, d), lambda s: (tile_c(s) // nj, 0), pipeline_mode=pl.Buffered(1)),
        ],
        out_specs=pl.BlockSpec((tm, d), lambda s: (tile_c(s) // nj, 0)),
        out_shape=jax.ShapeDtypeStruct((t, d), F32),
        scratch_shapes=[pltpu.VMEM((te, tm), F32), pltpu.VMEM((te, tm), F32),
                        pltpu.VMEM((te, tm), BF16), pltpu.VMEM((te, tm), BF16),
                        pltpu.VMEM((PEER_HEADS, nk, tm), F32), pltpu.VMEM((d, tm), F32)],
        compiler_params=pltpu.CompilerParams(
            dimension_semantics=("arbitrary",), vmem_limit_bytes=VMEM_LIMIT),
        name="peer",
    )(h2t, u_bf, vt_bf, s2, m2, th, e1, x1)


def _transpose_tiles_kernel(x_ref, o_ref):
    blk = o_ref.shape[1]
    for c0 in range(0, o_ref.shape[0], blk):
        o_ref[c0:c0 + blk, :] = x_ref[:, c0:c0 + blk].T.astype(o_ref.dtype)


def _transpose_tiles(x, dtype, *, te):
    r, c = x.shape
    return pl.pallas_call(
        _transpose_tiles_kernel,
        grid=(r // te,),
        in_specs=[pl.BlockSpec((te, c), lambda j: (j, 0))],
        out_specs=pl.BlockSpec((None, c, te), lambda j: (j, 0, 0)),
        out_shape=jax.ShapeDtypeStruct((r // te, c, te), dtype),
        compiler_params=pltpu.CompilerParams(
            dimension_semantics=("parallel",), vmem_limit_bytes=VMEM_LIMIT),
        name="transpose_tiles",
    )(x)


def _layer(x, norm_mix, w_in, q_norm, k_norm, sg_norm, sg_w, sg_b, out_norm_sb, out_norm_sg,
           w_out, norm_ffn, peer_w_query, peer_sub_keys, peer_u, peer_v):
    b, s, d = x.shape
    t = b * s
    x2 = x.reshape(t, d)
    row = lambda v: v.reshape(1, -1)

    proj = _inproj(x2, row(norm_mix), w_in.astype(BF16), row(q_norm), row(k_norm), row(sg_norm))
    y_sb = _attn(proj.reshape(b, s, -1)).reshape(t, -1)
    sg_b_bcast = jnp.broadcast_to(sg_b[:, :, None], sg_w.shape)
    x1, h2, h2t = _mix_out(y_sb, proj, sg_w, sg_b_bcast, x2, w_out.astype(BF16),
                           row(out_norm_sb), row(out_norm_sg), row(norm_ffn))

    s2, st5 = _query(h2, peer_w_query.astype(BF16), peer_sub_keys.astype(BF16))
    th, e1, m2 = _topk(st5)
    vt = _transpose_tiles(peer_v, BF16, te=PEER_EXPERT_TILE)
    out = _peer(h2t, peer_u.astype(BF16), vt, s2, m2, th, e1, x1)
    return out.reshape(b, s, d)


def kernel(x, norm_mix, w_in, q_norm, k_norm, sg_norm, sg_w, sg_b, out_norm_sb, out_norm_sg,
           w_out, norm_ffn, peer_w_query, peer_sub_keys, peer_u, peer_v):
    for layer in range(norm_mix.shape[0]):
        x = _layer(x, norm_mix[layer], w_in[layer], q_norm[layer], k_norm[layer], sg_norm[layer],
                   sg_w[layer], sg_b[layer], out_norm_sb[layer], out_norm_sg[layer], w_out[layer],
                   norm_ffn[layer], peer_w_query[layer], peer_sub_keys[layer], peer_u[layer],
                   peer_v[layer])
    return x
```

```python
import functools
import math

import jax
import jax.numpy as jnp
from jax import lax
from jax.experimental import pallas as pl
from jax.experimental.pallas import tpu as pltpu

F32 = jnp.float32
BF16 = jnp.bfloat16

RMS_EPS = 1e-6
LANES = 128
SUBLANES = 8
CHUNK = 64
SB_HEADS = 8
SG_GROUPS = 8
PEER_HEADS = 8
PEER_KEYS = 128
PEER_TOPK = 16
VMEM_LIMIT = 56 * 1024 * 1024

_NT = (((1,), (1,)), ((), ()))
_LOG2E = 1.0 / math.log(2.0)


def _gelu(x):
    return 0.5 * x * (1.0 + lax.erf(x * (1.0 / math.sqrt(2.0))))


def _rms(x):
    return x * lax.rsqrt(jnp.mean(x * x, axis=-1, keepdims=True) + RMS_EPS)


def _inproj_kernel(x_ref, nm_ref, w_ref, qn_ref, kn_ref, sgn_ref, o_ref, h_ref, *, tiles_per_region):
    j = pl.program_id(1)
    tn = o_ref.shape[1]

    @pl.when(j == 0)
    def _():
        h_ref[...] = (_rms(x_ref[...]) * nm_ref[...]).astype(BF16)

    region = j // tiles_per_region
    mxu_n = 2 * LANES

    def sweep(epilogue):
        for c0 in range(0, tn, mxu_n):
            acc = jnp.dot(h_ref[...], w_ref[:, c0:c0 + mxu_n], preferred_element_type=F32)
            for c in range(c0, c0 + mxu_n, LANES):
                sl = slice(c, c + LANES)
                o_ref[:, sl] = epilogue(acc[:, c - c0:c - c0 + LANES], sl).astype(o_ref.dtype)

    @pl.when(region == 0)
    def _():
        sweep(lambda a, sl: _rms(a) * qn_ref[...])

    @pl.when(region == 1)
    def _():
        sweep(lambda a, sl: _rms(a) * kn_ref[...])

    @pl.when(region == 2)
    def _():
        sweep(lambda a, sl: a)

    @pl.when(region == 3)
    def _():
        sweep(lambda a, sl: _gelu(a))

    @pl.when(region == 4)
    def _():
        sweep(lambda a, sl: _rms(_gelu(a)) * sgn_ref[:, sl])


def _inproj(x2, norm_mix, w_in, q_norm, k_norm, sg_norm, *, tm=1024, tn=1024):
    t, d = x2.shape
    n = w_in.shape[1]
    region_w = SB_HEADS * LANES
    tpr = region_w // tn
    return pl.pallas_call(
        functools.partial(_inproj_kernel, tiles_per_region=tpr),
        grid=(t // tm, n // tn),
        in_specs=[
            pl.BlockSpec((tm, d), lambda i, j: (i, 0)),
            pl.BlockSpec((1, d), lambda i, j: (0, 0)),
            pl.BlockSpec((d, tn), lambda i, j: (0, j)),
            pl.BlockSpec((1, LANES), lambda i, j: (0, 0)),
            pl.BlockSpec((1, LANES), lambda i, j: (0, 0)),
            pl.BlockSpec((1, tn), lambda i, j: (0, jnp.maximum(j - 4 * tpr, 0))),
        ],
        out_specs=pl.BlockSpec((tm, tn), lambda i, j: (i, j)),
        out_shape=jax.ShapeDtypeStruct((t, n), BF16),
        scratch_shapes=[pltpu.VMEM((tm, d), BF16)],
        compiler_params=pltpu.CompilerParams(
            dimension_semantics=("parallel", "arbitrary"), vmem_limit_bytes=VMEM_LIMIT),
        name="inproj",
    )(x2, norm_mix, w_in, q_norm, k_norm, sg_norm)


def _attn_kernel(q_ref, k_ref, v_ref, tri_ref, o_ref, acc_ref, c_ref, *, scale):
    i = pl.program_id(1)
    row = lax.broadcasted_iota(jnp.int32, (LANES, LANES), 0)
    col = lax.broadcasted_iota(jnp.int32, (LANES, LANES), 1)
    strict = col < row

    heads = range(SB_HEADS)
    cols = [slice(h * LANES, (h + 1) * LANES) for h in heads]

    def blocks(starts, diag):
        chains = [(start, h) for start in starts for h in heads]
        zs = [lax.dot_general(q_ref[0, :, cols[h]], k_ref[0, pl.ds(start, LANES), cols[h]], _NT,
                              preferred_element_type=F32) * (scale * _LOG2E) for start, h in chains]
        sps = [jnp.maximum(z, 0.0) + jnp.log2(1.0 + jnp.exp2(-jnp.abs(z))) for z in zs]
        if diag:
            sps = [jnp.where(strict, sp, 0.0) for sp in sps]
        css = []
        for sp in sps:
            hi = sp.astype(BF16)
            lo = (sp - hi.astype(F32)).astype(BF16)
            css.append(jnp.dot(jnp.concatenate([hi, lo], axis=1), tri_ref[...], preferred_element_type=F32))
        carried = [None if diag else c_ref[:, cols[h]] for h in heads]
        acc = [None if diag else acc_ref[:, cols[h]] for h in heads]
        for n, (start, h) in enumerate(chains):
            tail = css[n][:, :LANES]
            if carried[h] is not None:
                tail = tail + carried[h]
            w = jnp.exp2(zs[n] - sps[n] - tail)
            if diag:
                w = jnp.where(strict, w, 0.0)
            pv = jnp.dot(w.astype(BF16), v_ref[0, pl.ds(start, LANES), cols[h]], preferred_element_type=F32)
            acc[h] = pv if acc[h] is None else acc[h] + pv
            total = css[n][:, LANES:]
            carried[h] = total if carried[h] is None else carried[h] + total
        for h in heads:
            acc_ref[:, cols[h]] = acc[h]
            c_ref[:, cols[h]] = carried[h]

    blocks([pl.multiple_of(i * LANES, LANES)], True)

    def body(t, carry):
        right = pl.multiple_of((i - 1 - 2 * t) * LANES, LANES)
        blocks([right, pl.multiple_of(right - LANES, LANES)], False)
        return carry

    lax.fori_loop(0, i // 2, body, 0)

    @pl.when(i % 2 == 1)
    def _():
        blocks([0], False)

    o_ref[0] = acc_ref[...].astype(o_ref.dtype)


def _cumsum_matrix():
    j = lax.broadcasted_iota(jnp.int32, (2 * LANES, 2 * LANES), 0) % LANES
    s = lax.broadcasted_iota(jnp.int32, (2 * LANES, 2 * LANES), 1)
    return jnp.where((s >= LANES) | (j > s), 1.0, 0.0).astype(BF16)


def _attn(proj3):
    b, s, _ = proj3.shape
    w = SB_HEADS * LANES
    return pl.pallas_call(
        functools.partial(_attn_kernel, scale=LANES ** -0.5),
        grid=(b, s // LANES),
        in_specs=[
            pl.BlockSpec((1, LANES, w), lambda bi, i: (bi, i, 0)),
            pl.BlockSpec((1, s, w), lambda bi, i: (bi, 0, 1)),
            pl.BlockSpec((1, s, w), lambda bi, i: (bi, 0, 2)),
            pl.BlockSpec((2 * LANES, 2 * LANES), lambda bi, i: (0, 0)),
        ],
        out_specs=pl.BlockSpec((1, LANES, w), lambda bi, i: (bi, i, 0)),
        out_shape=jax.ShapeDtypeStruct((b, s, w), BF16),
        scratch_shapes=[pltpu.VMEM((LANES, w), F32), pltpu.VMEM((LANES, w), F32)],
        compiler_params=pltpu.CompilerParams(
            dimension_semantics=("parallel", "arbitrary"), vmem_limit_bytes=VMEM_LIMIT),
        name="attn",
    )(proj3, proj3, proj3, _cumsum_matrix())


def _mix_out_kernel(ysb_ref, u_ref, vb_ref, sgw_ref, sgb_ref, x_ref, wout_ref, gsb_ref, gsg_ref, nf_ref,
                    x1_ref, h2_ref, h2t_ref, ysg_ref):
    tm = x_ref.shape[0]
    p = lax.broadcasted_iota(jnp.int32, (LANES, LANES), 0) // CHUNK
    q = lax.broadcasted_iota(jnp.int32, (LANES, LANES), 1) // CHUNK
    causal = p >= q
    for g in range(SG_GROUPS):
        w = jnp.where(causal, sgw_ref[g], 0.0).astype(BF16)
        cols = slice(g * LANES, (g + 1) * LANES)
        for r in range(tm // LANES):
            rows = slice(r * LANES, (r + 1) * LANES)
            mixed = jnp.dot(w, vb_ref[rows, cols], preferred_element_type=F32) + sgb_ref[g]
            ysg_ref[rows, cols] = u_ref[rows, cols].astype(F32) * mixed
    sbw = ysb_ref.shape[1]
    y_sb = (_rms(ysb_ref[...].astype(F32)) * gsb_ref[...]).astype(BF16)
    y_sg = (_rms(ysg_ref[...]) * gsg_ref[...]).astype(BF16)
    y = jnp.dot(y_sb, wout_ref[:sbw, :], preferred_element_type=F32)
    y = y + jnp.dot(y_sg, wout_ref[sbw:, :], preferred_element_type=F32)
    x1 = x_ref[...] + y
    x1_ref[...] = x1
    h2 = _rms(x1) * nf_ref[...]
    h2_ref[...] = h2.astype(BF16)
    for c in range(0, h2.shape[1], LANES):
        h2t_ref[c:c + LANES, :] = h2[:, c:c + LANES].T.astype(BF16)


def _mix_out(y_sb, proj, sg_w, sg_b_bcast, x2, w_out, g_sb, g_sg, norm_ffn, *, tm=512):
    t, d = x2.shape
    sbw = y_sb.shape[1]
    sgw = SG_GROUPS * LANES
    u_blk = (3 * sbw) // sgw
    return pl.pallas_call(
        _mix_out_kernel,
        grid=(t // tm,),
        in_specs=[
            pl.BlockSpec((tm, sbw), lambda i: (i, 0)),
            pl.BlockSpec((tm, sgw), lambda i: (i, u_blk)),
            pl.BlockSpec((tm, sgw), lambda i: (i, u_blk + 1)),
            pl.BlockSpec((SG_GROUPS, LANES, LANES), lambda i: (0, 0, 0)),
            pl.BlockSpec((SG_GROUPS, LANES, LANES), lambda i: (0, 0, 0)),
            pl.BlockSpec((tm, d), lambda i: (i, 0)),
            pl.BlockSpec((sbw + sgw, d), lambda i: (0, 0)),
            pl.BlockSpec((1, sbw), lambda i: (0, 0)),
            pl.BlockSpec((1, sgw), lambda i: (0, 0)),
            pl.BlockSpec((1, d), lambda i: (0, 0)),
        ],
        out_specs=[pl.BlockSpec((tm, d), lambda i: (i, 0)), pl.BlockSpec((tm, d), lambda i: (i, 0)),
                   pl.BlockSpec((d, tm), lambda i: (0, i))],
        out_shape=[jax.ShapeDtypeStruct((t, d), F32), jax.ShapeDtypeStruct((t, d), BF16),
                   jax.ShapeDtypeStruct((d, t), BF16)],
        scratch_shapes=[pltpu.VMEM((tm, sgw), F32)],
        compiler_params=pltpu.CompilerParams(
            dimension_semantics=("parallel",), vmem_limit_bytes=VMEM_LIMIT),
        name="mix_out",
    )(y_sb, proj, proj, sg_w, sg_b_bcast, x2, w_out, g_sb, g_sg, norm_ffn)


def _query_kernel(h2_ref, wq_ref, keys_ref, s2_ref, st_ref):
    qf = jnp.dot(h2_ref[...], wq_ref[...], preferred_element_type=F32)
    hw = 2 * PEER_KEYS
    for h in range(PEER_HEADS):
        qn = _rms(qf[:, h * hw:(h + 1) * hw]).astype(BF16)
        for half in range(2):
            qk = qn[:, half * PEER_KEYS:(half + 1) * PEER_KEYS]
            s = lax.dot_general(keys_ref[half], qk, _NT, preferred_element_type=F32)
            if half == 1:
                s2_ref[h] = s
            for c in range(s.shape[1] // LANES):
                st_ref[h, half, :, c, :] = s[:, c * LANES:(c + 1) * LANES]


def _query(h2, w_query, sub_keys, *, tm=SUBLANES * LANES):
    t, d = h2.shape
    return pl.pallas_call(
        _query_kernel,
        grid=(t // tm,),
        in_specs=[
            pl.BlockSpec((tm, d), lambda i: (i, 0)),
            pl.BlockSpec(w_query.shape, lambda i: (0, 0), pipeline_mode=pl.Buffered(1)),
            pl.BlockSpec(sub_keys.shape, lambda i: (0, 0, 0)),
        ],
        out_specs=[
            pl.BlockSpec((PEER_HEADS, PEER_KEYS, tm), lambda i: (0, 0, i)),
            pl.BlockSpec((PEER_HEADS, 2, PEER_KEYS, tm // LANES, LANES), lambda i: (0, 0, 0, i, 0)),
        ],
        out_shape=[
            jax.ShapeDtypeStruct((PEER_HEADS, PEER_KEYS, t), F32),
            jax.ShapeDtypeStruct((PEER_HEADS, 2, PEER_KEYS, t // LANES, LANES), F32),
        ],
        compiler_params=pltpu.CompilerParams(
            dimension_semantics=("parallel",), vmem_limit_bytes=VMEM_LIMIT),
        name="query",
    )(h2, w_query, sub_keys)


def _sort_pairs(n):
    pairs = []
    p = 1
    while p < n:
        k = p
        while k >= 1:
            for j in range(k % p, n - k, 2 * k):
                for i in range(min(k, n - j - k)):
                    if (i + j) // (2 * p) == (i + j + k) // (2 * p):
                        pairs.append((i + j, i + j + k))
            k //= 2
        p *= 2
    return pairs


def _exchange(v, lo, hi):
    a, b = v[lo], v[hi]
    if b is None:
        return
    if a is None:
        v[lo], v[hi] = b, None
        return
    v[lo], v[hi] = jnp.maximum(a, b), jnp.minimum(a, b)


def _sort_desc(v):
    v = list(v)
    for lo, hi in _sort_pairs(len(v)):
        _exchange(v, lo, hi)
    return v


def _merge_top(x, y):
    n = len(x)
    v = [jnp.maximum(x[i], y[n - 1 - i]) for i in range(n)]
    k = n // 2
    while k >= 1:
        for i in range(n):
            if i & k == 0:
                _exchange(v, i, i + k)
        k //= 2
    return v


def _top_desc(vals, k):
    groups = [_sort_desc(vals[g:g + k]) for g in range(0, len(vals), k)]
    while len(groups) > 1:
        groups = [_merge_top(groups[g], groups[g + 1]) for g in range(0, len(groups), 2)]
    return groups[0]


def _topk_kernel(s_ref, th_ref, e1_ref, m2_ref):
    k = PEER_TOPK
    a = [s_ref[0, 0, n] for n in range(PEER_KEYS)]
    b = [s_ref[0, 1, n] for n in range(PEER_KEYS)]
    ta = _top_desc(a, k)
    tb = _top_desc(b, k)
    cand = [ta[p] + tb[q] for p in range(k) for q in range(k) if (p + 1) * (q + 1) <= k]
    n_sort = 1 << (len(cand) - 1).bit_length()
    tau = _sort_desc(cand + [None] * (n_sort - len(cand)))[k - 1]
    top = ta[0] + tb[0]
    z = jnp.zeros_like(top)
    for c in cand:
        z = z + jnp.where(c >= tau, jnp.exp(c - top), 0.0)
    half_inv_z = 0.5 / z
    m2_ref[0] = tb[0]
    for n in range(PEER_KEYS):
        th = jnp.full_like(top, jnp.inf)
        for q in range(k):
            th = jnp.where(a[n] + tb[q] >= tau, tb[q], th)
        th_ref[0, n] = th
        e1_ref[0, n] = jnp.exp(a[n] - ta[0]) * half_inv_z


def _topk(st5):
    hh, _, nk, nc, _ = st5.shape
    blk = (1, nk, SUBLANES, LANES)
    return pl.pallas_call(
        _topk_kernel,
        grid=(hh, nc // SUBLANES),
        in_specs=[pl.BlockSpec((1, 2, nk, SUBLANES, LANES), lambda h, u: (h, 0, 0, u, 0))],
        out_specs=[
            pl.BlockSpec(blk, lambda h, u: (h, 0, u, 0)),
            pl.BlockSpec(blk, lambda h, u: (h, 0, u, 0)),
            pl.BlockSpec((1, SUBLANES, LANES), lambda h, u: (h, u, 0)),
        ],
        out_shape=[
            jax.ShapeDtypeStruct((hh, nk, nc, LANES), F32),
            jax.ShapeDtypeStruct((hh, nk, nc, LANES), F32),
            jax.ShapeDtypeStruct((hh, nc, LANES), F32),
        ],
        compiler_params=pltpu.CompilerParams(
            dimension_semantics=("parallel", "parallel"), vmem_limit_bytes=VMEM_LIMIT),
        name="topk",
    )(st5)


_PACK_ROWS = 16
PEER_EXPERT_TILE = 512


def _peer_kernel(h2t_ref, u_ref, vt_ref, s2_ref, m2_ref, th_ref, e1_ref, x1_ref, o_ref,
                 st0_ref, st1_ref, w0_ref, w1_ref, e2_ref, acc_ref, *, nj, n_tiles):
    s = pl.program_id(0)
    st_ref = (st0_ref, st1_ref)
    w_ref = (w0_ref, w1_ref)
    te, tm = w0_ref.shape
    n_chunks = tm // LANES
    n_l = te // PEER_KEYS
    heads = range(PEER_HEADS)
    sb = jnp.clip(s - 1, 0, n_tiles - 1)
    sc = jnp.clip(s - 2, 0, n_tiles - 1)
    ib, jb = sb // nj, sb % nj
    chunk0 = (ib % (SUBLANES // n_chunks)) * n_chunks

    @pl.when(s == 0)
    def _():
        st1_ref[...] = jnp.zeros_like(st1_ref)
        w0_ref[...] = jnp.zeros_like(w0_ref)

    @pl.when(jb == 0)
    def _():
        for h in heads:
            for c in range(n_chunks):
                lanes = slice(c * LANES, (c + 1) * LANES)
                e2_ref[h, :, lanes] = jnp.exp(s2_ref[h, :, lanes] - m2_ref[h, pl.ds(chunk0 + c, 1), :])

    @pl.when(sc % nj == 0)
    def _():
        acc_ref[...] = jnp.zeros_like(acc_ref)

    d = acc_ref.shape[0]
    halves = (0, 1)

    def step(par):
        def scores(mh, nt):
            m = slice(mh * (te // 2), (mh + 1) * (te // 2))
            n = slice(nt * (tm // 2), (nt + 1) * (tm // 2))
            st_ref[par][m, n] = jnp.dot(u_ref[m, :], h2t_ref[:, n], preferred_element_type=F32)

        def accumulate(mq, nt):
            m = slice(mq * (d // 16), (mq + 1) * (d // 16))
            n = slice(nt * (tm // 2), (nt + 1) * (tm // 2))
            acc_ref[m, n] += jnp.dot(vt_ref[m, :], w_ref[par][:, n], preferred_element_type=F32)

        def gate_rows(c, r):
            lanes = slice(c * LANES, (c + 1) * LANES)
            keys = slice(r * _PACK_ROWS, (r + 1) * _PACK_ROWS)
            gates = [None] * n_l
            for h in heads:
                s2 = s2_ref[h, keys, lanes]
                e2 = e2_ref[h, keys, lanes]
                for l in range(n_l):
                    i1 = jb * n_l + l
                    th = th_ref[h, i1, pl.ds(chunk0 + c, 1), :]
                    e1 = e1_ref[h, i1, pl.ds(chunk0 + c, 1), :]
                    term = jnp.where(s2 >= th, e2, 0.0) * e1
                    gates[l] = term if gates[l] is None else gates[l] + term
            for l in range(n_l):
                rows = slice(l * PEER_KEYS + r * _PACK_ROWS, l * PEER_KEYS + (r + 1) * _PACK_ROWS)
                a = st_ref[1 - par][rows, lanes]
                w_ref[1 - par][rows, lanes] = (
                    a * (1.0 + lax.erf(a * (1.0 / math.sqrt(2.0)))) * gates[l]).astype(BF16)

        mxu_work = []
        for nt in halves:
            for mh in halves:
                mxu_work += [(functools.partial(scores, mh, nt), 0)]
                mxu_work += [(functools.partial(accumulate, 8 * mh + q, nt), 1) for q in range(8)]
        vpu_work = [functools.partial(gate_rows, c, r) for r in range(PEER_KEYS // _PACK_ROWS)
                    for c in range(n_chunks)]
        total = sum(share for _, share in mxu_work)
        done = issued = 0
        for mm, share in mxu_work:
            mm()
            issued += share
            upto = issued * len(vpu_work) // total
            for blk in vpu_work[done:upto]:
                blk()
            done = upto

    @pl.when(s % 2 == 0)
    def _():
        step(0)

    @pl.when(s % 2 == 1)
    def _():
        step(1)

    @pl.when((s >= 2) & (sc % nj == nj - 1))
    def _():
        for n in range(tm // LANES):
            rows = slice(n * LANES, (n + 1) * LANES)
            for m in range(d // LANES):
                cols = slice(m * LANES, (m + 1) * LANES)
                o_ref[rows, cols] = x1_ref[rows, cols] + acc_ref[cols, rows].T


def _peer(h2t, u_bf, vt_bf, s2, m2, th, e1, x1, *, tm=512):
    d, t = h2t.shape
    nk = PEER_KEYS
    nj, _, te = vt_bf.shape
    n_tiles = (t // tm) * nj
    tile_a = lambda s: jnp.minimum(s, n_tiles - 1)
    tile_b = lambda s: jnp.clip(s - 1, 0, n_tiles - 1)
    tile_c = lambda s: jnp.clip(s - 2, 0, n_tiles - 1)
    big = lambda s: (tile_b(s) // nj) * tm // (SUBLANES * LANES)
    return pl.pallas_call(
        functools.partial(_peer_kernel, nj=nj, n_tiles=n_tiles),
        grid=(n_tiles + 2,),
        in_specs=[
            pl.BlockSpec((d, tm), lambda s: (0, tile_a(s) // nj)),
            pl.BlockSpec((te, d), lambda s: (tile_a(s) % nj, 0)),
            pl.BlockSpec((None, d, te), lambda s: (tile_c(s) % nj, 0, 0)),
            pl.BlockSpec((PEER_HEADS, nk, tm), lambda s: (0, 0, tile_b(s) // nj)),
            pl.BlockSpec((PEER_HEADS, SUBLANES, LANES), lambda s: (0, big(s), 0)),
            pl.BlockSpec((PEER_HEADS, nk, SUBLANES, LANES), lambda s: (0, 0, big(s), 0)),
            pl.BlockSpec((PEER_HEADS, nk, SUBLANES, LANES), lambda s: (0, 0, big(s), 0)),
            pl.BlockSpec((tm, d), lambda s: (tile_c(s) // nj, 0), pipeline_mode=pl.Buffered(1)),
        ],
        out_specs=pl.BlockSpec((tm, d), lambda s: (tile_c(s) // nj, 0)),
        out_shape=jax.ShapeDtypeStruct((t, d), F32),
        scratch_shapes=[pltpu.VMEM((te, tm), F32), pltpu.VMEM((te, tm), F32),
                        pltpu.VMEM((te, tm), BF16), pltpu.VMEM((te, tm), BF16),
                        pltpu.VMEM((PEER_HEADS, nk, tm), F32), pltpu.VMEM((d, tm), F32)],
        compiler_params=pltpu.CompilerParams(
            dimension_semantics=("arbitrary",), vmem_limit_bytes=VMEM_LIMIT),
        name="peer",
    )(h2t, u_bf, vt_bf, s2, m2, th, e1, x1)


def _transpose_tiles_kernel(x_ref, o_ref):
    blk = o_ref.shape[1]
    for c0 in range(0, o_ref.shape[0], blk):
        o_ref[c0:c0 + blk, :] = x_ref[:, c0:c0 + blk].T.astype(o_ref.dtype)


def _transpose_tiles(x, dtype, *, te):
    r, c = x.shape
    return pl.pallas_call(
        _transpose_tiles_kernel,
        grid=(r // te,),
        in_specs=[pl.BlockSpec((te, c), lambda j: (j, 0))],
        out_specs=pl.BlockSpec((None, c, te), lambda j: (j, 0, 0)),
        out_shape=jax.ShapeDtypeStruct((r // te, c, te), dtype),
        compiler_params=pltpu.CompilerParams(
            dimension_semantics=("parallel",), vmem_limit_bytes=VMEM_LIMIT),
        name="transpose_tiles",
    )(x)


def _layer(x, norm_mix, w_in, q_norm, k_norm, sg_norm, sg_w, sg_b, out_norm_sb, out_norm_sg,
           w_out, norm_ffn, peer_w_query, peer_sub_keys, peer_u, peer_v):
    b, s, d = x.shape
    t = b * s
    x2 = x.reshape(t, d)
    row = lambda v: v.reshape(1, -1)

    proj = _inproj(x2, row(norm_mix), w_in.astype(BF16), row(q_norm), row(k_norm), row(sg_norm))
    y_sb = _attn(proj.reshape(b, s, -1)).reshape(t, -1)
    sg_b_bcast = jnp.broadcast_to(sg_b[:, :, None], sg_w.shape)
    x1, h2, h2t = _mix_out(y_sb, proj, sg_w, sg_b_bcast, x2, w_out.astype(BF16),
                           row(out_norm_sb), row(out_norm_sg), row(norm_ffn))

    s2, st5 = _query(h2, peer_w_query.astype(BF16), peer_sub_keys.astype(BF16))
    th, e1, m2 = _topk(st5)
    vt = _transpose_tiles(peer_v, BF16, te=PEER_EXPERT_TILE)
    out = _peer(h2t, peer_u.astype(BF16), vt, s2, m2, th, e1, x1)
    return out.reshape(b, s, d)


def kernel(x, norm_mix, w_in, q_norm, k_norm, sg_norm, sg_w, sg_b, out_norm_sb, out_norm_sg,
           w_out, norm_ffn, peer_w_query, peer_sub_keys, peer_u, peer_v):
    for layer in range(norm_mix.shape[0]):
        x = _layer(x, norm_mix[layer], w_in[layer], q_norm[layer], k_norm[layer], sg_norm[layer],
                   sg_w[layer], sg_b[layer], out_norm_sb[layer], out_norm_sg[layer], w_out[layer],
                   norm_ffn[layer], peer_w_query[layer], peer_sub_keys[layer], peer_u[layer],
                   peer_v[layer])
    return x
```
